```python
import jax, jax.numpy as jnp
from jax import lax
import numpy as np


D_MODEL = 1024
BATCH = 8
SEQ = 8192
DEPTH = 2

PLE_DIM = 256
D_FF = 2816
EPS = 1e-6
ROPE_THETA = 10000.0
RET_THETA = 10000.0
N_NORMS = 7

A_HEADS = 4
A_DK = 128
A_DV = 128
A_CHUNK = 64
B_Q_HEADS = 8
B_KV_HEADS = 2
B_HEAD_DIM = 64
B_WINDOW = 128
B_BLOCK = 128
C_HEADS = 4
C_DK = 256
C_DV = 512
C_CHUNK = 128

N_EVEN = (DEPTH + 1) // 2
N_ODD = DEPTH // 2

A_KW = A_HEADS * A_DK
A_VW = A_HEADS * A_DV
B_QW = B_Q_HEADS * B_HEAD_DIM
B_KW = B_KV_HEADS * B_HEAD_DIM
EVEN_SPLITS = [A_KW, 2 * A_KW, 2 * A_KW + A_VW, 2 * A_KW + 2 * A_VW, 2 * A_KW + 2 * A_VW + B_QW,
               2 * A_KW + 2 * A_VW + B_QW + B_KW]
EVEN_IN = 2 * A_KW + 2 * A_VW + B_QW + 2 * B_KW
EVEN_OUT = A_VW + B_QW
C_KW = C_HEADS * C_DK
C_VW = C_HEADS * C_DV
ODD_SPLITS = [C_KW, 2 * C_KW, 2 * C_KW + C_VW]
ODD_IN = 2 * C_KW + 2 * C_VW
ODD_OUT = C_VW

kernel_name = 'hybrid_hgrn2_swa_sink_retention_macaron'


def rmsnorm(x, g):
    xf = x.astype(jnp.float32)
    y = xf * lax.rsqrt(jnp.mean(xf * xf, axis=-1, keepdims=True) + EPS)
    return (y * g.astype(jnp.float32)).astype(x.dtype)


def swiglu(x, w_in, w_out):
    gate, up = jnp.split(x @ w_in, 2, axis=-1)
    return (jax.nn.silu(gate) * up) @ w_out


def rope_tables(positions, inv_freq):
    ang = positions.astype(jnp.float32)[:, :, None] * inv_freq[None, None, :]
    return jnp.cos(ang)[:, :, None, :], jnp.sin(ang)[:, :, None, :]


def apply_rope(x, cos, sin):
    xf = x.astype(jnp.float32)
    x1, x2 = jnp.split(xf, 2, axis=-1)
    return jnp.concatenate([x1 * cos - x2 * sin, x2 * cos + x1 * sin], axis=-1).astype(x.dtype)


def hgrn2_chunkwise(q, f_logit, i, lb):
    B, S, H, K = q.shape
    V = i.shape[-1]
    C = A_CHUNK
    n = S // C

    def chunks(t):
        return t.astype(jnp.float32).reshape(B, n, C, H, -1).transpose(0, 3, 1, 2, 4)

    lb5 = lb[None, :, None, None, :]
    qf = jax.nn.silu(chunks(q))
    f = lb5 + (1.0 - lb5) * jax.nn.sigmoid(chunks(f_logit))
    k = 1.0 - f
    v = chunks(i)
    b = jnp.cumsum(jnp.log(f), axis=3)
    b_last = b[:, :, :, -1:, :]
    q_s = qf * jnp.exp(b)
    k_s = k * jnp.exp(-b)
    k_d = k * jnp.exp(b_last - b)
    causal = jnp.tril(jnp.ones((C, C), dtype=bool))
    att = jnp.where(causal, jnp.einsum('bhnck,bhnsk->bhncs', q_s, k_s), 0.0)
    intra = jnp.einsum('bhncs,bhnsv->bhncv', att, v)
    dec = jnp.exp(b_last[:, :, :, 0, :])

    def step(state, xs):
        qc, kc, vc, dc = xs
        o = jnp.einsum('bhck,bhkv->bhcv', qc, state)
        state = state * dc[..., None] + jnp.einsum('bhck,bhcv->bhkv', kc, vc)
        return state, o

    xs = (jnp.moveaxis(q_s, 2, 0), jnp.moveaxis(k_d, 2, 0), jnp.moveaxis(v, 2, 0), jnp.moveaxis(dec, 2, 0))
    _, inter = lax.scan(step, jnp.zeros((B, H, K, V), jnp.float32), xs)
    o = intra + jnp.moveaxis(inter, 0, 2)
    return o.transpose(0, 2, 3, 1, 4).reshape(B, S, H, V)


def swa_with_sinks(q, k, v, sinks):
    B, S, Hq, D = q.shape
    G = k.shape[2]
    R = Hq // G
    L = B_BLOCK
    nb = S // L
    qb = q.reshape(B, nb, L, G, R, D)
    kb = k.reshape(B, nb, L, G, D)
    vb = v.reshape(B, nb, L, G, D)
    pad = ((0, 0), (1, 0), (0, 0), (0, 0), (0, 0))
    kk = jnp.concatenate([jnp.pad(kb[:, :-1], pad), kb], axis=2)
    vv = jnp.concatenate([jnp.pad(vb[:, :-1], pad), vb], axis=2)
    s = jnp.einsum('bnqgrd,bnkgd->bngrqk', qb, kk).astype(jnp.float32) * (D ** -0.5)
    qi = jnp.arange(L)[:, None]
    kj = jnp.arange(2 * L)[None, :]
    diff = qi + L - kj
    band = (diff >= 0) & (diff < B_WINDOW)
    key_pos = jnp.arange(nb)[:, None, None] * L - L + kj[None]
    valid = band[None] & (key_pos >= 0)
    s = jnp.where(valid[None, :, None, None], s, -jnp.inf)
    sink = sinks.astype(jnp.float32).reshape(G, R)[None, None, :, :, None, None]
    m = jnp.maximum(jnp.max(s, axis=-1, keepdims=True), sink)
    pr = jnp.exp(s - m)
    denom = jnp.sum(pr, axis=-1, keepdims=True) + jnp.exp(sink - m)
    o = jnp.einsum('bngrqk,bnkgd->bnqgrd', (pr / denom).astype(vv.dtype), vv)
    return o.reshape(B, S, Hq * D)


def retention_chunkwise(q, k, v):
    B, S, H, K = q.shape
    V = v.shape[-1]
    C = C_CHUNK
    n = S // C
    lg = jnp.log1p(-jnp.exp2(-5.0 - jnp.arange(H, dtype=jnp.float32)))

    def chunks(t):
        return t.astype(jnp.float32).reshape(B, n, C, H, -1).transpose(0, 3, 1, 2, 4)

    qc = chunks(q)
    kc = chunks(k) * (K ** -0.5)
    vc = chunks(v)
    idx = jnp.arange(C, dtype=jnp.float32)
    diff = idx[:, None] - idx[None, :]
    decay = jnp.where(diff >= 0, jnp.exp(jnp.maximum(diff, 0.0)[None] * lg[:, None, None]), 0.0)
    att = jnp.einsum('bhnck,bhnsk->bhncs', qc, kc) * decay[None, :, None]
    intra = jnp.einsum('bhncs,bhnsv->bhncv', att, vc)
    q_in = qc * jnp.exp((idx + 1.0)[None, :] * lg[:, None])[None, :, None, :, None]
    k_in = kc * jnp.exp((C - 1.0 - idx)[None, :] * lg[:, None])[None, :, None, :, None]
    chunk_dec = jnp.exp(C * lg)[None, :, None, None]

    def step(state, xs):
        qx, kx, vx = xs
        o = jnp.einsum('bhck,bhkv->bhcv', qx, state)
        state = state * chunk_dec + jnp.einsum('bhck,bhcv->bhkv', kx, vx)
        return state, o

    xs = (jnp.moveaxis(q_in, 2, 0), jnp.moveaxis(k_in, 2, 0), jnp.moveaxis(vc, 2, 0))
    _, inter = lax.scan(step, jnp.zeros((B, H, K, V), jnp.float32), xs)
    o = intra + jnp.moveaxis(inter, 0, 2)
    return o.transpose(0, 2, 3, 1, 4).reshape(B, S, H, V)


def even_mixer(u, w_in, w_out, lb, onorm_g, sinks, cos_b, sin_b):
    B, S, _ = u.shape
    qa, fa, ia, ga, qb, kb, vb = jnp.split(u @ w_in, EVEN_SPLITS, axis=-1)
    oa = hgrn2_chunkwise(qa.reshape(B, S, A_HEADS, A_DK), fa.reshape(B, S, A_HEADS, A_DK),
                         ia.reshape(B, S, A_HEADS, A_DV), lb.reshape(A_HEADS, A_DK))
    oa = rmsnorm(oa, onorm_g) * jax.nn.silu(ga.reshape(B, S, A_HEADS, A_DV).astype(jnp.float32))
    oa = oa.reshape(B, S, A_VW).astype(u.dtype)
    qh = apply_rope(qb.reshape(B, S, B_Q_HEADS, B_HEAD_DIM), cos_b, sin_b)
    kh = apply_rope(kb.reshape(B, S, B_KV_HEADS, B_HEAD_DIM), cos_b, sin_b)
    ob = swa_with_sinks(qh, kh, vb.reshape(B, S, B_KV_HEADS, B_HEAD_DIM), sinks).astype(u.dtype)
    return jnp.concatenate([oa, ob], axis=-1) @ w_out


def odd_mixer(u, w_in, w_out, onorm_g, cos_c, sin_c):
    B, S, _ = u.shape
    qc, kc, vc, gc = jnp.split(u @ w_in, ODD_SPLITS, axis=-1)
    qh = apply_rope(qc.reshape(B, S, C_HEADS, C_DK), cos_c, sin_c)
    kh = apply_rope(kc.reshape(B, S, C_HEADS, C_DK), cos_c, sin_c)
    o = retention_chunkwise(qh, kh, vc.reshape(B, S, C_HEADS, C_DV))
    o = rmsnorm(o, onorm_g) * jax.nn.silu(gc.reshape(B, S, C_HEADS, C_DV).astype(jnp.float32))
    return o.reshape(B, S, C_VW).astype(u.dtype) @ w_out


def setup_inputs(seed: int = 0) -> dict:
    key = jax.random.key(seed)
    ks = jax.random.split(key, 18)
    f32 = jnp.float32

    def w(k, shape, fan_in):
        return jax.random.normal(k, shape, f32) * (fan_in ** -0.5)

    x = jax.random.normal(ks[0], (BATCH, SEQ, D_MODEL), f32)
    p = jax.random.normal(ks[1], (DEPTH, BATCH, SEQ, PLE_DIM), f32)
    offsets = jax.random.randint(ks[2], (BATCH, 1), 0, 4096, dtype=jnp.int32)
    positions = (offsets + jnp.arange(SEQ, dtype=jnp.int32)[None, :]).astype(jnp.int32)
    norm_g = 1.0 + 0.05 * jax.random.normal(ks[3], (DEPTH, N_NORMS, D_MODEL), f32)
    ffn1_w_in = w(ks[4], (DEPTH, D_MODEL, 2 * D_FF), D_MODEL)
    ffn1_w_out = w(ks[5], (DEPTH, D_FF, D_MODEL), D_FF)
    ffn2_w_in = w(ks[6], (DEPTH, D_MODEL, 2 * D_FF), D_MODEL)
    ffn2_w_out = w(ks[7], (DEPTH, D_FF, D_MODEL), D_FF)
    ple_w_proj = w(ks[8], (DEPTH, PLE_DIM, D_MODEL), PLE_DIM)
    ple_w_gate = w(ks[9], (DEPTH, D_MODEL, D_MODEL), D_MODEL)
    even_w_in = w(ks[10], (N_EVEN, D_MODEL, EVEN_IN), D_MODEL)
    even_w_out = w(ks[11], (N_EVEN, EVEN_OUT, D_MODEL), EVEN_OUT)
    hgrn_lb = 0.1 * jax.random.normal(ks[12], (N_EVEN + 1, A_KW), f32)
    hgrn_onorm_g = 1.0 + 0.05 * jax.random.normal(ks[13], (N_EVEN, A_DV), f32)
    attn_sinks = jax.random.normal(ks[14], (N_EVEN, B_Q_HEADS), f32)
    odd_w_in = w(ks[15], (N_ODD, D_MODEL, ODD_IN), D_MODEL)
    odd_w_out = w(ks[16], (N_ODD, ODD_OUT, D_MODEL), ODD_OUT)
    ret_onorm_g = 1.0 + 0.05 * jax.random.normal(ks[17], (N_ODD, C_DV), f32)
    return {'x': x, 'p': p, 'positions': positions, 'norm_g': norm_g,
            'ffn1_w_in': ffn1_w_in, 'ffn1_w_out': ffn1_w_out, 'ffn2_w_in': ffn2_w_in, 'ffn2_w_out': ffn2_w_out,
            'ple_w_proj': ple_w_proj, 'ple_w_gate': ple_w_gate,
            'even_w_in': even_w_in, 'even_w_out': even_w_out, 'hgrn_lb': hgrn_lb, 'hgrn_onorm_g': hgrn_onorm_g,
            'attn_sinks': attn_sinks, 'odd_w_in': odd_w_in, 'odd_w_out': odd_w_out, 'ret_onorm_g': ret_onorm_g}


def reference(x, p, positions, norm_g, ffn1_w_in, ffn1_w_out, ffn2_w_in, ffn2_w_out, ple_w_proj, ple_w_gate,
              even_w_in, even_w_out, hgrn_lb, hgrn_onorm_g, attn_sinks, odd_w_in, odd_w_out, ret_onorm_g):
    inv_b = ROPE_THETA ** (-jnp.arange(0, B_HEAD_DIM, 2, dtype=jnp.float32) / B_HEAD_DIM)
    inv_c = RET_THETA ** (-jnp.linspace(0.0, 1.0, C_DK // 2, dtype=jnp.float32))
    cos_b, sin_b = rope_tables(positions, inv_b)
    cos_c, sin_c = rope_tables(positions, inv_c)
    lb_all = jnp.cumsum(jax.nn.softmax(hgrn_lb.astype(jnp.float32), axis=0), axis=0)
    h = x
    for i in range(DEPTH):
        g = norm_g[i]
        h = h + 0.5 * rmsnorm(swiglu(rmsnorm(h, g[0]), ffn1_w_in[i], ffn1_w_out[i]), g[1])
        u = rmsnorm(h, g[2])
        j = i // 2
        if i % 2 == 0:
            mix = even_mixer(u, even_w_in[j], even_w_out[j], lb_all[j], hgrn_onorm_g[j], attn_sinks[j], cos_b, sin_b)
        else:
            mix = odd_mixer(u, odd_w_in[j], odd_w_out[j], ret_onorm_g[j], cos_c, sin_c)
        h = h + rmsnorm(mix, g[3])
        h = h + 0.5 * rmsnorm(swiglu(rmsnorm(h, g[4]), ffn2_w_in[i], ffn2_w_out[i]), g[5])
        gate = jax.nn.sigmoid(h @ ple_w_gate[i])
        h = h + rmsnorm(gate * (p[i] @ ple_w_proj[i]), g[6])
    return h
```

```python
import functools

import numpy as np
import jax
import jax.numpy as jnp
from jax import lax
from jax.experimental import pallas as pl
from jax.experimental.pallas import tpu as pltpu

F32 = jnp.float32
BF16 = jnp.bfloat16

EPS = 1e-6
ROPE_THETA = 10000.0
RET_THETA = 10000.0

A_HEADS, A_DK, A_DV, A_CHUNK = 4, 128, 128, 64
B_Q_HEADS, B_KV_HEADS, B_HEAD_DIM, B_WINDOW, B_BLOCK = 8, 2, 64, 128, 128
C_HEADS, C_DK, C_DV = 4, 256, 512
C_CHUNK = 256

A_KW = A_HEADS * A_DK
A_VW = A_HEADS * A_DV
B_QW = B_Q_HEADS * B_HEAD_DIM
B_KW = B_KV_HEADS * B_HEAD_DIM
C_KW = C_HEADS * C_DK
C_VW = C_HEADS * C_DV

VMEM_LIMIT_BYTES = 56 * 1024 * 1024

NT_DIMS = (((1,), (1,)), ((), ()))
TN_DIMS = (((0,), (0,)), ((), ()))


def _rmsnorm(x, g):
    return x * lax.rsqrt(jnp.mean(x * x, axis=-1, keepdims=True) + EPS) * g


def _sigmoid(x):
    return 1.0 / (1.0 + jnp.exp(-x))


def _silu(x):
    return x * _sigmoid(x)


def _dot(a, b):
    return jnp.dot(a, b, preferred_element_type=F32)


def _params(*sem):
    return pltpu.CompilerParams(dimension_semantics=sem, vmem_limit_bytes=VMEM_LIMIT_BYTES)


def _resident(shape):
    return pl.BlockSpec(shape, lambda *_: (0,) * len(shape), pipeline_mode=pl.Buffered(1))


def _ffn_body(h_ref, ga_ref, gb_ref, win_ref, wout_ref, o_ref, act_ref, *, d_ff, fc):
    h = h_ref[...]
    x = _rmsnorm(h, ga_ref[...]).astype(BF16)
    for j in range(d_ff // fc):
        gate = _dot(x, win_ref[:, j * fc:(j + 1) * fc])
        up = _dot(x, win_ref[:, d_ff + j * fc:d_ff + (j + 1) * fc])
        act_ref[:, j * fc:(j + 1) * fc] = (_silu(gate) * up).astype(BF16)
    y = _dot(act_ref[...], wout_ref[...])
    o_ref[...] = h + 0.5 * _rmsnorm(y, gb_ref[...])


def _ffn(h, ga, gb, w_in, w_out, *, tm, interpret):
    t, d = h.shape
    d_ff = w_out.shape[0]
    fc = 256
    return pl.pallas_call(
        functools.partial(_ffn_body, d_ff=d_ff, fc=fc),
        grid=(t // tm,),
        in_specs=[pl.BlockSpec((tm, d), lambda i: (i, 0)), _resident((1, d)), _resident((1, d)),
                  _resident((d, 2 * d_ff)), _resident((d_ff, d))],
        out_specs=pl.BlockSpec((tm, d), lambda i: (i, 0)),
        out_shape=jax.ShapeDtypeStruct((t, d), F32),
        scratch_shapes=[pltpu.VMEM((tm, d_ff), BF16)],
        compiler_params=_params("parallel"),
        interpret=interpret, name="ffn",
    )(h, ga, gb, w_in, w_out)


def _inproj_body(h_ref, g_ref, w_ref, o_ref, *, nc):
    x = _rmsnorm(h_ref[...], g_ref[...]).astype(BF16)
    n = w_ref.shape[1]
    for j in range(n // nc):
        o_ref[:, j * nc:(j + 1) * nc] = _dot(x, w_ref[:, j * nc:(j + 1) * nc]).astype(o_ref.dtype)


def _inproj(h, g, w, *, tm, nc, out_dtype, interpret):
    t, d = h.shape
    n = w.shape[1]
    return pl.pallas_call(
        functools.partial(_inproj_body, nc=nc),
        grid=(t // tm,),
        in_specs=[pl.BlockSpec((tm, d), lambda i: (i, 0)), _resident((1, d)), _resident((d, n))],
        out_specs=pl.BlockSpec((tm, n), lambda i: (i, 0)),
        out_shape=jax.ShapeDtypeStruct((t, n), out_dtype),
        compiler_params=_params("parallel"),
        interpret=interpret, name="inproj",
    )(h, g, w)


def _outproj_body(*refs, n_in):
    h_ref, g_ref = refs[0], refs[1]
    o_refs = refs[2:2 + n_in]
    w_refs = refs[2 + n_in:2 + 2 * n_in]
    out_ref = refs[2 + 2 * n_in]
    acc = _dot(o_refs[0][...], w_refs[0][...])
    for o_ref, w_ref in zip(o_refs[1:], w_refs[1:]):
        acc = acc + _dot(o_ref[...], w_ref[...])
    out_ref[...] = h_ref[...] + _rmsnorm(acc, g_ref[...])


def _outproj(h, g, os_, ws, *, tm, interpret):
    t, d = h.shape
    n_in = len(os_)
    return pl.pallas_call(
        functools.partial(_outproj_body, n_in=n_in),
        grid=(t // tm,),
        in_specs=([pl.BlockSpec((tm, d), lambda i: (i, 0)), _resident((1, d))]
                  + [pl.BlockSpec((tm, o.shape[1]), lambda i: (i, 0)) for o in os_]
                  + [_resident(w.shape) for w in ws]),
        out_specs=pl.BlockSpec((tm, d), lambda i: (i, 0)),
        out_shape=jax.ShapeDtypeStruct((t, d), F32),
        compiler_params=_params("parallel"),
        interpret=interpret, name="outproj",
    )(h, g, *os_, *ws)


def _ple_body(h_ref, p_ref, g_ref, wg_ref, wp_ref, o_ref):
    h = h_ref[...]
    gate = _sigmoid(_dot(h.astype(BF16), wg_ref[...]))
    proj = _dot(p_ref[...].astype(BF16), wp_ref[...])
    o_ref[...] = h + _rmsnorm(gate * proj, g_ref[...])


def _ple(h, p, g, w_gate, w_proj, *, tm, interpret):
    t, d = h.shape
    dp = p.shape[1]
    return pl.pallas_call(
        _ple_body,
        grid=(t // tm,),
        in_specs=[pl.BlockSpec((tm, d), lambda i: (i, 0)), pl.BlockSpec((tm, dp), lambda i: (i, 0)),
                  _resident((1, d)), _resident((d, d)), _resident((dp, d))],
        out_specs=pl.BlockSpec((tm, d), lambda i: (i, 0)),
        out_shape=jax.ShapeDtypeStruct((t, d), F32),
        compiler_params=_params("parallel"),
        interpret=interpret, name="ple",
    )(h, p, g, w_gate, w_proj)


def _hgrn_body(q_ref, f_ref, i_ref, g_ref, lb_ref, og_ref, o_ref, st_ref, *, n_chunks):
    c = A_CHUNK

    @pl.when(pl.program_id(1) == 0)
    def _():
        st_ref[...] = jnp.zeros_like(st_ref)

    row = lax.broadcasted_iota(jnp.int32, (c, c), 0)
    col = lax.broadcasted_iota(jnp.int32, (c, c), 1)
    causal = row >= col
    ones_tril = causal.astype(BF16)
    lb = lb_ref[...]
    og = og_ref[...]

    def chunk(ci, carry):
        r0 = pl.multiple_of(ci * c, c)
        rows = pl.ds(r0, c)
        q = _silu(q_ref[rows, :])
        f = lb + (1.0 - lb) * _sigmoid(f_ref[rows, :])
        k = 1.0 - f
        v = i_ref[rows, :]
        gate = _silu(g_ref[rows, :])
        logf = jnp.log(f)
        logf_hi = logf.astype(BF16)
        logf_lo = (logf - logf_hi.astype(F32)).astype(BF16)
        b = _dot(ones_tril, logf_hi) + _dot(ones_tril, logf_lo)
        b_last = b[c - 1:c, :]
        q_s = (q * jnp.exp(b)).astype(BF16)
        k_s = (k * jnp.exp(-b)).astype(BF16)
        k_d = (k * jnp.exp(b_last - b)).astype(BF16)
        dec = jnp.exp(b_last)
        vb = v.astype(BF16)
        outs = []
        for hh in range(A_HEADS):
            sl = slice(hh * A_DK, (hh + 1) * A_DK)
            att = lax.dot_general(q_s[:, sl], k_s[:, sl], NT_DIMS, preferred_element_type=F32)
            att = jnp.where(causal, att, 0.0).astype(BF16)
            st = st_ref[hh]
            o = _dot(att, vb[:, sl]) + lax.dot_general(q_s[:, sl], st.astype(BF16), NT_DIMS,
                                                       preferred_element_type=F32)
            upd = lax.dot_general(vb[:, sl], k_d[:, sl], TN_DIMS, preferred_element_type=F32)
            st_ref[hh] = st * dec[:, sl] + upd
            outs.append(_rmsnorm(o, og) * gate[:, sl])
        o_ref[rows, :] = jnp.concatenate(outs, axis=-1).astype(o_ref.dtype)
        return carry

    lax.fori_loop(0, n_chunks, chunk, 0)


def _hgrn(y, lb, og, *, batch, seq, ts, interpret):
    ns = seq // ts
    w = A_KW

    def col(cb):
        return pl.BlockSpec((ts, w), lambda b, s: (b * ns + s, cb))

    return pl.pallas_call(
        functools.partial(_hgrn_body, n_chunks=ts // A_CHUNK),
        grid=(batch, ns),
        in_specs=[col(0), col(1), col(2), col(3), _resident((1, w)), _resident((1, A_DV))],
        out_specs=pl.BlockSpec((ts, A_VW), lambda b, s: (b * ns + s, 0)),
        out_shape=jax.ShapeDtypeStruct((batch * seq, A_VW), BF16),
        scratch_shapes=[pltpu.VMEM((A_HEADS, A_DV, A_DK), F32)],
        compiler_params=_params("parallel", "arbitrary"),
        interpret=interpret, name="hgrn2",
    )(y, y, y, y, lb, og)


def _rope64(x, cos, sin_signed):
    n = x.shape[1]
    lane = lax.broadcasted_iota(jnp.int32, x.shape, 1)
    half = B_HEAD_DIM // 2
    swapped = jnp.where((lane % B_HEAD_DIM) < half, pltpu.roll(x, n - half, 1), pltpu.roll(x, half, 1))
    reps = n // cos.shape[1]
    cos_t = jnp.concatenate([cos] * reps, axis=1) if reps > 1 else cos
    sin_t = jnp.concatenate([sin_signed] * reps, axis=1) if reps > 1 else sin_signed
    return x * cos_t + swapped * sin_t


def _swa_body(sink_ref, q_ref, kc_ref, kp_ref, vc_ref, vp_ref, cosc_ref, sinc_ref, cosp_ref, sinp_ref, o_ref):
    blk = B_BLOCK
    n = pl.program_id(1)
    q = _rope64(q_ref[...], cosc_ref[...], sinc_ref[...]) * (B_HEAD_DIM ** -0.5)
    kk = jnp.concatenate([_rope64(kp_ref[...], cosp_ref[...], sinp_ref[...]),
                          _rope64(kc_ref[...], cosc_ref[...], sinc_ref[...])], axis=0)
    vv = jnp.concatenate([vp_ref[...], vc_ref[...]], axis=0)
    lane = lax.broadcasted_iota(jnp.int32, kk.shape, 1)
    low = lane < B_HEAD_DIM
    kk_sw = pltpu.roll(kk, B_HEAD_DIM, 1)
    vv_sw = pltpu.roll(vv, B_HEAD_DIM, 1)

    def padded(x, x_sw, g, pos):
        src = x if g == pos else x_sw
        return jnp.where(low if pos == 0 else ~low, src, 0.0).astype(BF16)

    qi = lax.broadcasted_iota(jnp.int32, (blk, 2 * blk), 0)
    kj = lax.broadcasted_iota(jnp.int32, (blk, 2 * blk), 1)
    diff = qi + blk - kj
    valid = (diff >= 0) & (diff < B_WINDOW) & ((n * blk - blk + kj) >= 0)

    rep = B_Q_HEADS // B_KV_HEADS
    for j in range(B_QW // 128):
        qj = q[:, j * 128:(j + 1) * 128].astype(BF16)
        acc = None
        for pos in range(2):
            head = 2 * j + pos
            g = head // rep
            s = lax.dot_general(qj, padded(kk, kk_sw, g, pos), NT_DIMS, preferred_element_type=F32)
            s = jnp.where(valid, s, -jnp.inf)
            sink = sink_ref[head]
            m = jnp.maximum(jnp.max(s, axis=-1, keepdims=True), sink)
            pr = jnp.exp(s - m)
            denom = jnp.sum(pr, axis=-1, keepdims=True) + jnp.exp(sink - m)
            part = _dot((pr / denom).astype(BF16), padded(vv, vv_sw, g, pos))
            acc = part if acc is None else acc + part
        o_ref[:, j * 128:(j + 1) * 128] = acc.astype(o_ref.dtype)


def _swa(y, sinks, cos, sin_signed, *, batch, seq, interpret):
    blk = B_BLOCK
    nb = seq // blk
    q_cb = (2 * A_KW + 2 * A_VW) // B_QW
    k_cb = (2 * A_KW + 2 * A_VW + B_QW) // B_KW
    v_cb = k_cb + 1

    def cur(width, cb):
        return pl.BlockSpec((blk, width), lambda b, n: (b * nb + n, cb))

    def prev(width, cb):
        return pl.BlockSpec((blk, width), lambda b, n: (b * nb + jnp.maximum(n - 1, 0), cb))

    return pl.pallas_call(
        _swa_body,
        grid=(batch, nb),
        in_specs=[pl.BlockSpec(memory_space=pltpu.SMEM),
                  cur(B_QW, q_cb), cur(B_KW, k_cb), prev(B_KW, k_cb), cur(B_KW, v_cb), prev(B_KW, v_cb),
                  cur(128, 0), cur(128, 0), prev(128, 0), prev(128, 0)],
        out_specs=pl.BlockSpec((blk, B_QW), lambda b, n: (b * nb + n, 0)),
        out_shape=jax.ShapeDtypeStruct((batch * seq, B_QW), BF16),
        compiler_params=_params("parallel", "arbitrary"),
        interpret=interpret, name="swa",
    )(sinks, y, y, y, y, y, cos, sin_signed, cos, sin_signed)


def _retention_body(q_ref, k_ref, v_ref, g_ref, cos_ref, sin_ref, og_ref, o_ref, st_ref, *, log_gammas):
    c = C_CHUNK
    half = C_DK // 2

    @pl.when(pl.program_id(1) == 0)
    def _():
        st_ref[...] = jnp.zeros_like(st_ref)

    cos = cos_ref[...]
    sin = sin_ref[...]
    og = og_ref[...]
    row = lax.broadcasted_iota(jnp.int32, (c, c), 0)
    col = lax.broadcasted_iota(jnp.int32, (c, c), 1)
    diff = (row - col).astype(F32)
    rowf = row.astype(F32)

    def rope(x):
        x1, x2 = x[:, :half], x[:, half:]
        return jnp.concatenate([x1 * cos - x2 * sin, x2 * cos + x1 * sin], axis=-1)

    for hh, lg in enumerate(log_gammas):
        qr = rope(q_ref[:, hh * C_DK:(hh + 1) * C_DK].astype(F32))
        kr = rope(k_ref[:, hh * C_DK:(hh + 1) * C_DK].astype(F32)) * (C_DK ** -0.5)
        v = v_ref[:, hh * C_DV:(hh + 1) * C_DV]
        decay = jnp.where(diff >= 0, jnp.exp(jnp.maximum(diff, 0.0) * lg), 0.0)
        att = lax.dot_general(qr.astype(BF16), kr.astype(BF16), NT_DIMS, preferred_element_type=F32) * decay
        q_in = (qr * jnp.exp((rowf + 1.0) * lg)).astype(BF16)
        k_in = (kr * jnp.exp((c - 1.0 - rowf) * lg)).astype(BF16)
        st = st_ref[hh]
        o = _dot(att.astype(BF16), v) + _dot(q_in, st.astype(BF16))
        st_ref[hh] = st * float(np.exp(c * lg)) + lax.dot_general(
            k_in, v, TN_DIMS, preferred_element_type=F32)
        gate = _silu(g_ref[:, hh * C_DV:(hh + 1) * C_DV].astype(F32))
        o_ref[:, hh * C_DV:(hh + 1) * C_DV] = (_rmsnorm(o, og) * gate).astype(o_ref.dtype)


def _retention(y, cos, sin, og, *, batch, seq, interpret):
    c = C_CHUNK
    ns = seq // c
    log_gammas = tuple(float(v) for v in np.log1p(-np.exp2(-5.0 - np.arange(C_HEADS, dtype=np.float64))))

    def blk(width, cb):
        return pl.BlockSpec((c, width), lambda b, s: (b * ns + s, cb))

    return pl.pallas_call(
        functools.partial(_retention_body, log_gammas=log_gammas),
        grid=(batch, ns),
        in_specs=[blk(C_KW, 0), blk(C_KW, 1), blk(C_VW, 1), blk(C_VW, 2), blk(C_DK // 2, 0), blk(C_DK // 2, 0),
                  _resident((1, C_DV))],
        out_specs=blk(C_VW, 0),
        out_shape=jax.ShapeDtypeStruct((batch * seq, C_VW), BF16),
        scratch_shapes=[pltpu.VMEM((C_HEADS, C_DK, C_DV), F32)],
        compiler_params=_params("parallel", "arbitrary"),
        interpret=interpret, name="retention",
    )(y, y, y, y, cos, sin, og)


def _forward(x, p, positions, norm_g, ffn1_w_in, ffn1_w_out, ffn2_w_in, ffn2_w_out, ple_w_proj, ple_w_gate,
             even_w_in, even_w_out, hgrn_lb, hgrn_onorm_g, attn_sinks, odd_w_in, odd_w_out, ret_onorm_g,
             *, interpret=False):
    batch, seq, d = x.shape
    t = batch * seq
    depth = norm_g.shape[0]
    tm = min(512, t)
    kw = dict(interpret=interpret)

    pos = positions.astype(F32).reshape(t, 1)
    inv_b = ROPE_THETA ** (-jnp.arange(0, B_HEAD_DIM, 2, dtype=F32) / B_HEAD_DIM)
    inv_c = RET_THETA ** (-jnp.linspace(0.0, 1.0, C_DK // 2, dtype=F32))
    ang_b = pos * inv_b[None, :]
    ang_c = pos * inv_c[None, :]
    cos_b = jnp.tile(jnp.cos(ang_b), (1, 4))
    sin_b = jnp.tile(jnp.concatenate([-jnp.sin(ang_b), jnp.sin(ang_b)], axis=1), (1, 2))
    cos_c, sin_c = jnp.cos(ang_c), jnp.sin(ang_c)
    lb_all = jnp.cumsum(jax.nn.softmax(hgrn_lb.astype(F32), axis=0), axis=0)

    h = x.reshape(t, d)
    for i in range(depth):
        g = norm_g[i][:, None, :]
        j = i // 2
        h = _ffn(h, g[0], g[1], ffn1_w_in[i].astype(BF16), ffn1_w_out[i].astype(BF16), tm=tm, **kw)
        if i % 2 == 0:
            y = _inproj(h, g[2], even_w_in[j].astype(BF16), tm=tm, nc=even_w_in.shape[2] // 2, out_dtype=F32, **kw)
            oa = _hgrn(y, lb_all[j][None, :], hgrn_onorm_g[j][None, :], batch=batch, seq=seq,
                       ts=min(512, seq), **kw)
            ob = _swa(y, attn_sinks[j].astype(F32), cos_b, sin_b, batch=batch, seq=seq, **kw)
            w_out = even_w_out[j].astype(BF16)
            h = _outproj(h, g[3], [oa, ob], [w_out[:A_VW], w_out[A_VW:]], tm=tm, **kw)
        else:
            y = _inproj(h, g[2], odd_w_in[j].astype(BF16), tm=tm, nc=C_KW, out_dtype=BF16, **kw)
            o = _retention(y, cos_c, sin_c, ret_onorm_g[j][None, :], batch=batch, seq=seq, **kw)
            h = _outproj(h, g[3], [o], [odd_w_out[j].astype(BF16)], tm=tm, **kw)
        h = _ffn(h, g[4], g[5], ffn2_w_in[i].astype(BF16), ffn2_w_out[i].astype(BF16), tm=tm, **kw)
        h = _ple(h, p[i].reshape(t, -1), g[6], ple_w_gate[i].astype(BF16), ple_w_proj[i].astype(BF16), tm=tm, **kw)
    return h.reshape(batch, seq, d)


def kernel(x, p, positions, norm_g, ffn1_w_in, ffn1_w_out, ffn2_w_in, ffn2_w_out, ple_w_proj, ple_w_gate,
           even_w_in, even_w_out, hgrn_lb, hgrn_onorm_g, attn_sinks, odd_w_in, odd_w_out, ret_onorm_g):
    return _forward(x, p, positions, norm_g, ffn1_w_in, ffn1_w_out, ffn2_w_in, ffn2_w_out, ple_w_proj, ple_w_gate,
                    even_w_in, even_w_out, hgrn_lb, hgrn_onorm_g, attn_sinks, odd_w_in, odd_w_out, ret_onorm_g)
```

```python
import functools

import numpy as np
import jax
import jax.numpy as jnp
from jax import lax
from jax.experimental import pallas as pl
from jax.experimental.pallas import tpu as pltpu

F32 = jnp.float32
BF16 = jnp.bfloat16

EPS = 1e-6
ROPE_THETA = 10000.0
RET_THETA = 10000.0
LOG2E = 1.4426950408889634

A_HEADS, A_DK, A_DV, A_CHUNK = 4, 128, 128, 64
B_Q_HEADS, B_KV_HEADS, B_HEAD_DIM, B_WINDOW, B_BLOCK = 8, 2, 64, 128, 128
C_HEADS, C_DK, C_DV = 4, 256, 512
C_CHUNK = 256

A_KW = A_HEADS * A_DK
A_VW = A_HEADS * A_DV
B_QW = B_Q_HEADS * B_HEAD_DIM
B_KW = B_KV_HEADS * B_HEAD_DIM
C_KW = C_HEADS * C_DK
C_VW = C_HEADS * C_DV

LANES = 128
VMEM_LIMIT_BYTES = 56 * 1024 * 1024

NT_DIMS = (((1,), (1,)), ((), ()))
TN_DIMS = (((0,), (0,)), ((), ()))


def _rmsnorm(x, g):
    return x * lax.rsqrt(jnp.mean(x * x, axis=-1, keepdims=True) + EPS) * g


def _sigmoid(x):
    return 1.0 / (1.0 + jnp.exp(-x))


def _silu(x):
    return x * _sigmoid(x)


def _dot(a, b):
    return jnp.dot(a, b, preferred_element_type=F32)


def _dot_nt(a, b):
    return lax.dot_general(a, b, NT_DIMS, preferred_element_type=F32)


def _dot_tn(a, b):
    return lax.dot_general(a, b, TN_DIMS, preferred_element_type=F32)


def _params(*sem):
    return pltpu.CompilerParams(dimension_semantics=sem, vmem_limit_bytes=VMEM_LIMIT_BYTES)


def _resident(shape):
    return pl.BlockSpec(shape, lambda *_: (0,) * len(shape), pipeline_mode=pl.Buffered(1))


def _rows(tm, width):
    return pl.BlockSpec((tm, width), lambda i: (i, 0))


def _swiglu(x, win_ref, wout_ref, act_ref, fc):
    d_ff = wout_ref.shape[0]
    for j in range(d_ff // fc):
        gate = _dot(x, win_ref[:, j * fc:(j + 1) * fc])
        up = _dot(x, win_ref[:, d_ff + j * fc:d_ff + (j + 1) * fc])
        act_ref[:, j * fc:(j + 1) * fc] = (_silu(gate) * up).astype(BF16)
    return _dot(act_ref[...], wout_ref[...])


def _block_body(*refs, n_mix, with_ple, ga, gb, fc):
    it = iter(refs)
    h_ref, g_ref = next(it), next(it)
    o_refs = [next(it) for _ in range(n_mix)]
    wo_refs = [next(it) for _ in range(n_mix)]
    win_ref, wout_ref = next(it), next(it)
    if with_ple:
        p_ref, wg_ref, wp_ref = next(it), next(it), next(it)
    out_ref, act_ref = next(it), next(it)

    def gain(k):
        return g_ref[k:k + 1, :]

    h = h_ref[...]
    if n_mix:
        mix = _dot(o_refs[0][...], wo_refs[0][...])
        for o_ref, w_ref in zip(o_refs[1:], wo_refs[1:]):
            mix = mix + _dot(o_ref[...], w_ref[...])
        h = h + _rmsnorm(mix, gain(3))
    y = _swiglu(_rmsnorm(h, gain(ga)).astype(BF16), win_ref, wout_ref, act_ref, fc)
    h = h + 0.5 * _rmsnorm(y, gain(gb))
    if with_ple:
        gate = _sigmoid(_dot(h.astype(BF16), wg_ref[...]))
        proj = _dot(p_ref[...].astype(BF16), wp_ref[...])
        h = h + _rmsnorm(gate * proj, gain(6))
    out_ref[...] = h


def _block(h, gains, w_in, w_out, *, ga, gb, mix=(), ple=None, tm, interpret):
    t, d = h.shape
    d_ff = w_out.shape[0]
    os_ = [o for o, _ in mix]
    wos = [w for _, w in mix]
    args = [h, gains, *os_, *wos, w_in, w_out]
    specs = ([_rows(tm, d), _resident(gains.shape)] + [_rows(tm, o.shape[1]) for o in os_]
             + [_resident(w.shape) for w in wos] + [_resident(w_in.shape), _resident(w_out.shape)])
    if ple is not None:
        p_all, layer, w_gate, w_proj = ple
        args += [p_all, w_gate, w_proj]
        specs += [pl.BlockSpec((None, tm, p_all.shape[2]), lambda i: (layer, i, 0)),
                  _resident(w_gate.shape), _resident(w_proj.shape)]
    return pl.pallas_call(
        functools.partial(_block_body, n_mix=len(mix), with_ple=ple is not None, ga=ga, gb=gb, fc=256),
        grid=(t // tm,),
        in_specs=specs,
        out_specs=_rows(tm, d),
        out_shape=jax.ShapeDtypeStruct((t, d), F32),
        scratch_shapes=[pltpu.VMEM((tm, d_ff), BF16)],
        compiler_params=_params("parallel"),
        interpret=interpret, name="block",
    )(*args)


def _inproj_body(*refs, n_out, nc):
    h_ref, g_ref = refs[0], refs[1]
    w_refs = refs[2:2 + n_out]
    o_refs = refs[2 + n_out:2 + 2 * n_out]
    x = _rmsnorm(h_ref[...], g_ref[2:3, :]).astype(BF16)
    for w_ref, o_ref in zip(w_refs, o_refs):
        n = w_ref.shape[1]
        step = min(nc, n)
        for j in range(n // step):
            o_ref[:, j * step:(j + 1) * step] = _dot(x, w_ref[:, j * step:(j + 1) * step]).astype(o_ref.dtype)


def _inproj(h, gains, ws, out_dtypes, *, tm, nc, interpret):
    t, d = h.shape
    return pl.pallas_call(
        functools.partial(_inproj_body, n_out=len(ws), nc=nc),
        grid=(t // tm,),
        in_specs=[_rows(tm, d), _resident(gains.shape)] + [_resident(w.shape) for w in ws],
        out_specs=[_rows(tm, w.shape[1]) for w in ws],
        out_shape=[jax.ShapeDtypeStruct((t, w.shape[1]), dt) for w, dt in zip(ws, out_dtypes)],
        compiler_params=_params("parallel"),
        interpret=interpret, name="inproj",
    )(h, gains, *ws)


def _hgrn_body(q_ref, f_ref, i_ref, g_ref, lb_ref, og_ref, o_ref, st_ref, qs_ref, kd_ref, acc_ref, dec_ref):
    ts = q_ref.shape[0]
    c, grp = A_CHUNK, 2 * LANES
    per_grp = grp // c

    @pl.when(pl.program_id(1) == 0)
    def _():
        st_ref[...] = jnp.zeros_like(st_ref)

    row = lax.broadcasted_iota(jnp.int32, (grp, grp), 0)
    col = lax.broadcasted_iota(jnp.int32, (grp, grp), 1)
    causal = ((row // c) == (col // c)) & (row >= col)
    ones_tril = causal.astype(BF16)
    lb = lb_ref[...]

    for gi in range(ts // grp):
        rows = slice(gi * grp, (gi + 1) * grp)
        q = _silu(q_ref[rows, :].astype(F32))
        f = lb + (1.0 - lb) * _sigmoid(f_ref[rows, :])
        k = 1.0 - f
        logf = jnp.log(f)
        logf_hi = logf.astype(BF16)
        logf_lo = (logf - logf_hi.astype(F32)).astype(BF16)
        b = _dot(ones_tril, logf_hi) + _dot(ones_tril, logf_lo)
        q_s = (q * jnp.exp(b)).astype(BF16)
        k_sf = k * jnp.exp(-b)
        k_s = k_sf.astype(BF16)
        decs = [jnp.exp(b[j * c + c - 1:j * c + c, :]) for j in range(per_grp)]
        dec_rows = jnp.concatenate([jnp.broadcast_to(dj, (c, dj.shape[1])) for dj in decs], axis=0)
        kd_ref[rows, :] = (k_sf * dec_rows).astype(BF16)
        qs_ref[rows, :] = q_s
        for j, dj in enumerate(decs):
            dec_ref[gi * per_grp + j:gi * per_grp + j + 1, :] = dj
        vb = i_ref[rows, :].astype(BF16)
        for hh in range(A_HEADS):
            sl = slice(hh * A_DK, (hh + 1) * A_DK)
            att = jnp.where(causal, _dot_nt(q_s[:, sl], k_s[:, sl]), 0.0).astype(BF16)
            acc_ref[rows, sl] = _dot(att, vb[:, sl])

    for hh in range(A_HEADS):
        sl = slice(hh * A_DK, (hh + 1) * A_DK)
        st = st_ref[hh]
        for j in range(ts // c):
            rws = slice(j * c, (j + 1) * c)
            acc_ref[rws, sl] += _dot_nt(qs_ref[rws, sl], st.astype(BF16))
            upd = _dot_tn(i_ref[rws, sl].astype(BF16), kd_ref[rws, sl])
            st = st * dec_ref[j:j + 1, sl] + upd
        st_ref[hh] = st

    og = og_ref[...]
    for hh in range(A_HEADS):
        sl = slice(hh * A_DK, (hh + 1) * A_DK)
        o_ref[:, sl] = (_rmsnorm(acc_ref[:, sl], og) * _silu(g_ref[:, sl].astype(F32))).astype(o_ref.dtype)


def _hgrn(y, fl, lb, og, *, batch, seq, ts, interpret):
    ns = seq // ts
    w = A_KW

    def col(cb):
        return pl.BlockSpec((ts, w), lambda b, s: (b * ns + s, cb))

    return pl.pallas_call(
        _hgrn_body,
        grid=(batch, ns),
        in_specs=[col(0), col(0), col(1), col(2), _resident((1, w)), _resident((1, A_DV))],
        out_specs=col(0),
        out_shape=jax.ShapeDtypeStruct((batch * seq, A_VW), BF16),
        scratch_shapes=[pltpu.VMEM((A_HEADS, A_DV, A_DK), F32), pltpu.VMEM((ts, w), BF16),
                        pltpu.VMEM((ts, w), BF16), pltpu.VMEM((ts, A_VW), F32),
                        pltpu.VMEM((ts // A_CHUNK, w), F32)],
        compiler_params=_params("parallel", "arbitrary"),
        interpret=interpret, name="hgrn2",
    )(y, fl, y, y, lb, og)


def _rope64(x, cos, sin_signed):
    n = x.shape[1]
    lane = lax.broadcasted_iota(jnp.int32, x.shape, 1)
    half = B_HEAD_DIM // 2
    swapped = jnp.where((lane % B_HEAD_DIM) < half, pltpu.roll(x, n - half, 1), pltpu.roll(x, half, 1))
    reps = n // cos.shape[1]
    cos_t = jnp.concatenate([cos] * reps, axis=1) if reps > 1 else cos
    sin_t = jnp.concatenate([sin_signed] * reps, axis=1) if reps > 1 else sin_signed
    return x * cos_t + swapped * sin_t


def _swa_body(sink_ref, q_ref, kc_ref, kp_ref, vc_ref, vp_ref, cosc_ref, sinc_ref, cosp_ref, sinp_ref, o_ref,
              kpad_ref, vpad_ref):
    blk = B_BLOCK
    tq = q_ref.shape[0]
    rep = B_Q_HEADS // B_KV_HEADS
    q = _rope64(q_ref[...].astype(F32), cosc_ref[...], sinc_ref[...]) * (B_HEAD_DIM ** -0.5 * LOG2E)
    kk = jnp.concatenate([_rope64(kp_ref[...].astype(F32), cosp_ref[...], sinp_ref[...]),
                          _rope64(kc_ref[...].astype(F32), cosc_ref[...], sinc_ref[...])], axis=0)
    vv = jnp.concatenate([vp_ref[...], vc_ref[...]], axis=0).astype(F32)
    low = lax.broadcasted_iota(jnp.int32, kk.shape, 1) < B_HEAD_DIM
    kk_sw = pltpu.roll(kk, B_HEAD_DIM, 1)
    vv_sw = pltpu.roll(vv, B_HEAD_DIM, 1)
    for g in range(B_KV_HEADS):
        for pos in range(2):
            keep = low if pos == 0 else ~low
            kpad_ref[2 * g + pos] = jnp.where(keep, kk if g == pos else kk_sw, 0.0).astype(BF16)
            vpad_ref[2 * g + pos] = jnp.where(keep, vv if g == pos else vv_sw, 0.0).astype(BF16)

    ones = jnp.ones((2 * blk, LANES), BF16)
    qi = lax.broadcasted_iota(jnp.int32, (2 * blk, 2 * blk), 0) % blk
    kj = lax.broadcasted_iota(jnp.int32, (2 * blk, 2 * blk), 1)
    diff = qi + blk - kj
    band = (diff >= 0) & (diff < B_WINDOW)
    band_first = band & ((kj + (pl.program_id(1) * tq - blk)) >= 0)
    upper = lax.broadcasted_iota(jnp.int32, (2 * blk, 1), 0) < blk

    for n in range(tq // blk):
        rows = slice(n * blk, (n + 1) * blk)
        keys = slice(n * blk, (n + 2) * blk)
        valid = band_first if n == 0 else band
        outs = [None] * (B_QW // LANES)
        for g in range(B_KV_HEADS):
            for pos in range(2):
                groups = (2 * g, 2 * g + 1)
                qs = jnp.concatenate([q[rows, j * LANES:(j + 1) * LANES] for j in groups], axis=0).astype(BF16)
                s = jnp.where(valid, _dot_nt(qs, kpad_ref[2 * g + pos, keys, :]), -jnp.inf)
                sink = jnp.where(upper, sink_ref[rep * g + pos], sink_ref[rep * g + 2 + pos]) * LOG2E
                m = jnp.maximum(jnp.max(s, axis=-1, keepdims=True), sink)
                pr = jnp.exp2(s - m).astype(BF16)
                den = _dot(pr, ones) + jnp.exp2(sink - m)
                res = _dot(pr, vpad_ref[2 * g + pos, keys, :]) / den
                for i, j in enumerate(groups):
                    part = res[i * blk:(i + 1) * blk]
                    outs[j] = part if outs[j] is None else outs[j] + part
        for j, oj in enumerate(outs):
            o_ref[rows, j * LANES:(j + 1) * LANES] = oj.astype(o_ref.dtype)


def _swa(y, sinks, cos, sin_signed, *, batch, seq, tq, q_cb, k_cb, v_cb, interpret):
    blk = B_BLOCK
    nt = seq // tq
    per = tq // blk

    def cur(width, cb):
        return pl.BlockSpec((tq, width), lambda b, n: (b * nt + n, cb))

    def prev(width, cb):
        return pl.BlockSpec((blk, width), lambda b, n: (jnp.maximum((b * nt + n) * per - 1, 0), cb))

    return pl.pallas_call(
        _swa_body,
        grid=(batch, nt),
        in_specs=[pl.BlockSpec(memory_space=pltpu.SMEM),
                  cur(B_QW, q_cb), cur(B_KW, k_cb), prev(B_KW, k_cb), cur(B_KW, v_cb), prev(B_KW, v_cb),
                  cur(LANES, 0), cur(LANES, 0), prev(LANES, 0), prev(LANES, 0)],
        out_specs=cur(B_QW, 0),
        out_shape=jax.ShapeDtypeStruct((batch * seq, B_QW), BF16),
        scratch_shapes=[pltpu.VMEM((4, tq + blk, B_KW), BF16), pltpu.VMEM((4, tq + blk, B_KW), BF16)],
        compiler_params=_params("parallel", "arbitrary"),
        interpret=interpret, name="swa",
    )(sinks, y, y, y, y, y, cos, sin_signed, cos, sin_signed)


def _retention_body(q_ref, k_ref, v_ref, g_ref, cos_ref, sin_ref, og_ref, o_ref, st_ref, *, log_gammas):
    c = C_CHUNK
    half = C_DK // 2

    @pl.when(pl.program_id(1) == 0)
    def _():
        st_ref[...] = jnp.zeros_like(st_ref)

    cos = cos_ref[...]
    sin = sin_ref[...]
    og = og_ref[...]
    row = lax.broadcasted_iota(jnp.int32, (c, c), 0)
    col = lax.broadcasted_iota(jnp.int32, (c, c), 1)
    diff = (row - col).astype(F32)
    rowf = row.astype(F32)

    def rope(x):
        x1, x2 = x[:, :half], x[:, half:]
        return jnp.concatenate([x1 * cos - x2 * sin, x2 * cos + x1 * sin], axis=-1)

    for hh, lg in enumerate(log_gammas):
        qr = rope(q_ref[:, hh * C_DK:(hh + 1) * C_DK].astype(F32))
        kr = rope(k_ref[:, hh * C_DK:(hh + 1) * C_DK].astype(F32)) * (C_DK ** -0.5)
        v = v_ref[:, hh * C_DV:(hh + 1) * C_DV]
        decay = jnp.where(diff >= 0, jnp.exp(jnp.maximum(diff, 0.0) * lg), 0.0)
        att = _dot_nt(qr.astype(BF16), kr.astype(BF16)) * decay
        q_in = (qr * jnp.exp((rowf + 1.0) * lg)).astype(BF16)
        k_in = (kr * jnp.exp((c - 1.0 - rowf) * lg)).astype(BF16)
        st = st_ref[hh]
        o = _dot(att.astype(BF16), v) + _dot(q_in, st.astype(BF16))
        st_ref[hh] = st * float(np.exp(c * lg)) + _dot_tn(k_in, v)
        gate = _silu(g_ref[:, hh * C_DV:(hh + 1) * C_DV].astype(F32))
        o_ref[:, hh * C_DV:(hh + 1) * C_DV] = (_rmsnorm(o, og) * gate).astype(o_ref.dtype)


def _retention(y, cos, sin, og, *, batch, seq, interpret):
    c = C_CHUNK
    ns = seq // c
    log_gammas = tuple(float(v) for v in np.log1p(-np.exp2(-5.0 - np.arange(C_HEADS, dtype=np.float64))))

    def blk(width, cb):
        return pl.BlockSpec((c, width), lambda b, s: (b * ns + s, cb))

    return pl.pallas_call(
        functools.partial(_retention_body, log_gammas=log_gammas),
        grid=(batch, ns),
        in_specs=[blk(C_KW, 0), blk(C_KW, 1), blk(C_VW, 1), blk(C_VW, 2), blk(C_DK // 2, 0), blk(C_DK // 2, 0),
                  _resident((1, C_DV))],
        out_specs=blk(C_VW, 0),
        out_shape=jax.ShapeDtypeStruct((batch * seq, C_VW), BF16),
        scratch_shapes=[pltpu.VMEM((C_HEADS, C_DK, C_DV), F32)],
        compiler_params=_params("parallel", "arbitrary"),
        interpret=interpret, name="retention",
    )(y, y, y, y, cos, sin, og)


def _forward(x, p, positions, norm_g, ffn1_w_in, ffn1_w_out, ffn2_w_in, ffn2_w_out, ple_w_proj, ple_w_gate,
             even_w_in, even_w_out, hgrn_lb, hgrn_onorm_g, attn_sinks, odd_w_in, odd_w_out, ret_onorm_g,
             *, interpret=False):
    batch, seq, d = x.shape
    t = batch * seq
    depth = norm_g.shape[0]
    tm = min(512, t)
    kw = dict(interpret=interpret)

    pos = positions.astype(F32).reshape(t, 1)
    inv_b = ROPE_THETA ** (-jnp.arange(0, B_HEAD_DIM, 2, dtype=F32) / B_HEAD_DIM)
    inv_c = RET_THETA ** (-jnp.linspace(0.0, 1.0, C_DK // 2, dtype=F32))
    ang_b = pos * inv_b[None, :]
    ang_c = pos * inv_c[None, :]
    cos_b = jnp.tile(jnp.cos(ang_b), (1, 4))
    sin_b = jnp.tile(jnp.concatenate([-jnp.sin(ang_b), jnp.sin(ang_b)], axis=1), (1, 2))
    cos_c, sin_c = jnp.cos(ang_c), jnp.sin(ang_c)
    lb_all = jnp.cumsum(jax.nn.softmax(hgrn_lb.astype(F32), axis=0), axis=0)
    p_all = p.reshape(depth, t, p.shape[-1])

    h = x.reshape(t, d)
    for i in range(depth):
        gains = norm_g[i]
        j = i // 2
        h = _block(h, gains, ffn1_w_in[i].astype(BF16), ffn1_w_out[i].astype(BF16), ga=0, gb=1, tm=tm, **kw)
        if i % 2 == 0:
            w_in = even_w_in[j].astype(BF16)
            w_f = w_in[:, A_KW:2 * A_KW]
            w_rest = jnp.concatenate([w_in[:, :A_KW], w_in[:, 2 * A_KW:]], axis=1)
            fl, y = _inproj(h, gains, [w_f, w_rest], [F32, BF16], tm=tm, nc=1152, **kw)
            oa = _hgrn(y, fl, lb_all[j][None, :], hgrn_onorm_g[j][None, :], batch=batch, seq=seq,
                       ts=min(512, seq), **kw)
            q_cb = (A_KW + 2 * A_VW) // B_QW
            k_cb = (A_KW + 2 * A_VW + B_QW) // B_KW
            ob = _swa(y, attn_sinks[j].astype(F32), cos_b, sin_b, batch=batch, seq=seq, tq=min(512, seq),
                      q_cb=q_cb, k_cb=k_cb, v_cb=k_cb + 1, **kw)
            w_out = even_w_out[j].astype(BF16)
            mix = [(oa, w_out[:A_VW]), (ob, w_out[A_VW:])]
        else:
            (y,) = _inproj(h, gains, [odd_w_in[j].astype(BF16)], [BF16], tm=tm, nc=C_KW, **kw)
            o = _retention(y, cos_c, sin_c, ret_onorm_g[j][None, :], batch=batch, seq=seq, **kw)
            mix = [(o, odd_w_out[j].astype(BF16))]
        h = _block(h, gains, ffn2_w_in[i].astype(BF16), ffn2_w_out[i].astype(BF16), ga=4, gb=5, mix=mix,
                   ple=(p_all, i, ple_w_gate[i].astype(BF16), ple_w_proj[i].astype(BF16)), tm=tm, **kw)
    return h.reshape(batch, seq, d)


def kernel(x, p, positions, norm_g, ffn1_w_in, ffn1_w_out, ffn2_w_in, ffn2_w_out, ple_w_proj, ple_w_gate,
           even_w_in, even_w_out, hgrn_lb, hgrn_onorm_g, attn_sinks, odd_w_in, odd_w_out, ret_onorm_g):
    return _forward(x, p, positions, norm_g, ffn1_w_in, ffn1_w_out, ffn2_w_in, ffn2_w_out, ple_w_proj, ple_w_gate,
                    even_w_in, even_w_out, hgrn_lb, hgrn_onorm_g, attn_sinks, odd_w_in, odd_w_out, ret_onorm_g)
```

```python
import functools

import numpy as np
import jax
import jax.numpy as jnp
from jax import lax
from jax.experimental import pallas as pl
from jax.experimental.pallas import tpu as pltpu

F32 = jnp.float32
BF16 = jnp.bfloat16

EPS = 1e-6
ROPE_THETA = 10000.0
RET_THETA = 10000.0
LOG2E = 1.4426950408889634

A_HEADS, A_DK, A_DV, A_CHUNK = 4, 128, 128, 64
B_Q_HEADS, B_KV_HEADS, B_HEAD_DIM, B_WINDOW, B_BLOCK = 8, 2, 64, 128, 128
C_HEADS, C_DK, C_DV = 4, 256, 512
C_CHUNK = 256

A_KW = A_HEADS * A_DK
A_VW = A_HEADS * A_DV
B_QW = B_Q_HEADS * B_HEAD_DIM
B_KW = B_KV_HEADS * B_HEAD_DIM
C_KW = C_HEADS * C_DK
C_VW = C_HEADS * C_DV

LANES = 128
VMEM_LIMIT_BYTES = 56 * 1024 * 1024

NT_DIMS = (((1,), (1,)), ((), ()))
TN_DIMS = (((0,), (0,)), ((), ()))


def _rmsnorm(x, g):
    return x * lax.rsqrt(jnp.mean(x * x, axis=-1, keepdims=True) + EPS) * g


def _sigmoid(x):
    return 1.0 / (1.0 + jnp.exp(-x))


def _silu(x):
    return x * _sigmoid(x)


def _dot(a, b):
    return jnp.dot(a, b, preferred_element_type=F32)


def _dot_nt(a, b):
    return lax.dot_general(a, b, NT_DIMS, preferred_element_type=F32)


def _dot_tn(a, b):
    return lax.dot_general(a, b, TN_DIMS, preferred_element_type=F32)


def _params(*sem):
    return pltpu.CompilerParams(dimension_semantics=sem, vmem_limit_bytes=VMEM_LIMIT_BYTES)


def _resident(shape):
    return pl.BlockSpec(shape, lambda *_: (0,) * len(shape), pipeline_mode=pl.Buffered(1))


def _rows(tm, width):
    return pl.BlockSpec((tm, width), lambda i: (i, 0))


def _swiglu(x, win_ref, wout_ref, act_ref, fc):
    d_ff = wout_ref.shape[0]
    for j in range(d_ff // fc):
        gate = _dot(x, win_ref[:, j * fc:(j + 1) * fc])
        up = _dot(x, win_ref[:, d_ff + j * fc:d_ff + (j + 1) * fc])
        act_ref[:, j * fc:(j + 1) * fc] = (_silu(gate) * up).astype(BF16)
    return _dot(act_ref[...], wout_ref[...])


def _block_body(*refs, n_mix, with_ple, ga, gb, fc):
    it = iter(refs)
    h_ref, g_ref = next(it), next(it)
    o_refs = [next(it) for _ in range(n_mix)]
    wo_refs = [next(it) for _ in range(n_mix)]
    win_ref, wout_ref = next(it), next(it)
    if with_ple:
        p_ref, wg_ref, wp_ref = next(it), next(it), next(it)
    out_ref, act_ref = next(it), next(it)

    def gain(k):
        return g_ref[k:k + 1, :]

    h = h_ref[...]
    if n_mix:
        mix = _dot(o_refs[0][...], wo_refs[0][...])
        for o_ref, w_ref in zip(o_refs[1:], wo_refs[1:]):
            mix = mix + _dot(o_ref[...], w_ref[...])
        h = h + _rmsnorm(mix, gain(3))
    y = _swiglu(_rmsnorm(h, gain(ga)).astype(BF16), win_ref, wout_ref, act_ref, fc)
    h = h + 0.5 * _rmsnorm(y, gain(gb))
    if with_ple:
        gate = _sigmoid(_dot(h.astype(BF16), wg_ref[...]))
        proj = _dot(p_ref[...].astype(BF16), wp_ref[...])
        h = h + _rmsnorm(gate * proj, gain(6))
    out_ref[...] = h


def _block(h, gains, w_in, w_out, *, ga, gb, mix=(), ple=None, tm, interpret):
    t, d = h.shape
    d_ff = w_out.shape[0]
    os_ = [o for o, _ in mix]
    wos = [w for _, w in mix]
    args = [h, gains, *os_, *wos, w_in, w_out]
    specs = ([_rows(tm, d), _resident(gains.shape)] + [_rows(tm, o.shape[1]) for o in os_]
             + [_resident(w.shape) for w in wos] + [_resident(w_in.shape), _resident(w_out.shape)])
    if ple is not None:
        p_all, layer, w_gate, w_proj = ple
        args += [p_all, w_gate, w_proj]
        specs += [pl.BlockSpec((None, tm, p_all.shape[2]), lambda i: (layer, i, 0)),
                  _resident(w_gate.shape), _resident(w_proj.shape)]
    return pl.pallas_call(
        functools.partial(_block_body, n_mix=len(mix), with_ple=ple is not None, ga=ga, gb=gb, fc=256),
        grid=(t // tm,),
        in_specs=specs,
        out_specs=_rows(tm, d),
        out_shape=jax.ShapeDtypeStruct((t, d), F32),
        scratch_shapes=[pltpu.VMEM((tm, d_ff), BF16)],
        compiler_params=_params("parallel"),
        interpret=interpret, name="block",
    )(*args)


def _rope64(x, cos, sin_signed):
    n = x.shape[1]
    lane = lax.broadcasted_iota(jnp.int32, x.shape, 1)
    half = B_HEAD_DIM // 2
    swapped = jnp.where((lane % B_HEAD_DIM) < half, pltpu.roll(x, n - half, 1), pltpu.roll(x, half, 1))
    reps = n // cos.shape[1]
    cos_t = jnp.concatenate([cos] * reps, axis=1) if reps > 1 else cos
    sin_t = jnp.concatenate([sin_signed] * reps, axis=1) if reps > 1 else sin_signed
    return x * cos_t + swapped * sin_t


def _inproj_even_body(h_ref, g_ref, w_ref, lb_ref, cos_ref, sin_ref, y_ref, logf_ref):
    x = _rmsnorm(h_ref[...], g_ref[2:3, :]).astype(BF16)
    w = A_KW

    def proj(lo, hi):
        return _dot(x, w_ref[:, lo:hi])

    def put(lo, val):
        y_ref[:, lo:lo + val.shape[1]] = val.astype(y_ref.dtype)

    put(0, _silu(proj(0, w)))
    lb = lb_ref[...]
    f = lb + (1.0 - lb) * _sigmoid(proj(w, 2 * w))
    logf_ref[...] = jnp.log(f)
    put(w, 1.0 - f)
    put(2 * w, proj(2 * w, 3 * w))
    put(3 * w, _silu(proj(3 * w, 4 * w)))
    cos, sin = cos_ref[...], sin_ref[...]
    put(4 * w, _rope64(proj(4 * w, 4 * w + B_QW), cos, sin) * (B_HEAD_DIM ** -0.5 * LOG2E))
    kv = proj(4 * w + B_QW, 4 * w + B_QW + 2 * B_KW)
    put(4 * w + B_QW, _rope64(kv[:, :B_KW], cos, sin))
    put(4 * w + B_QW + B_KW, kv[:, B_KW:])


def _inproj_even(h, gains, w, lb, cos, sin, *, tm, interpret):
    t, d = h.shape
    n = w.shape[1]
    return pl.pallas_call(
        _inproj_even_body,
        grid=(t // tm,),
        in_specs=[_rows(tm, d), _resident(gains.shape), _resident(w.shape), _resident(lb.shape),
                  _rows(tm, LANES), _rows(tm, LANES)],
        out_specs=[_rows(tm, n), _rows(tm, A_KW)],
        out_shape=[jax.ShapeDtypeStruct((t, n), BF16), jax.ShapeDtypeStruct((t, A_KW), F32)],
        compiler_params=_params("parallel"),
        interpret=interpret, name="inproj_even",
    )(h, gains, w, lb, cos, sin)


def _inproj_odd_body(h_ref, g_ref, w_ref, cos_ref, sin_ref, y_ref):
    x = _rmsnorm(h_ref[...], g_ref[2:3, :]).astype(BF16)
    cos, sin = cos_ref[...], sin_ref[...]
    half = C_DK // 2

    def proj(lo, hi):
        return _dot(x, w_ref[:, lo:hi])

    def rope(v):
        x1, x2 = v[:, :half], v[:, half:]
        return jnp.concatenate([x1 * cos - x2 * sin, x2 * cos + x1 * sin], axis=-1)

    for hh in range(C_HEADS):
        lo = hh * C_DK
        y_ref[:, lo:lo + C_DK] = rope(proj(lo, lo + C_DK)).astype(y_ref.dtype)
        lo = C_KW + hh * C_DK
        y_ref[:, lo:lo + C_DK] = (rope(proj(lo, lo + C_DK)) * (C_DK ** -0.5)).astype(y_ref.dtype)
    step = 1024
    for lo in range(2 * C_KW, 2 * C_KW + C_VW, step):
        y_ref[:, lo:lo + step] = proj(lo, lo + step).astype(y_ref.dtype)
    for lo in range(2 * C_KW + C_VW, 2 * C_KW + 2 * C_VW, step):
        y_ref[:, lo:lo + step] = _silu(proj(lo, lo + step)).astype(y_ref.dtype)


def _inproj_odd(h, gains, w, cos, sin, *, tm, interpret):
    t, d = h.shape
    n = w.shape[1]
    return pl.pallas_call(
        _inproj_odd_body,
        grid=(t // tm,),
        in_specs=[_rows(tm, d), _resident(gains.shape), _resident(w.shape), _rows(tm, LANES), _rows(tm, LANES)],
        out_specs=_rows(tm, n),
        out_shape=jax.ShapeDtypeStruct((t, n), BF16),
        compiler_params=_params("parallel"),
        interpret=interpret, name="inproj_odd",
    )(h, gains, w, cos, sin)


def _hgrn_body(q_ref, k_ref, v_ref, g_ref, lf_ref, og_ref, o_ref, st_ref, qs_ref, kd_ref, acc_ref, dec_ref):
    ts = q_ref.shape[0]
    c, grp = A_CHUNK, 2 * LANES
    per_grp = grp // c

    @pl.when(pl.program_id(1) == 0)
    def _():
        st_ref[...] = jnp.zeros_like(st_ref)

    row = lax.broadcasted_iota(jnp.int32, (grp, grp), 0)
    col = lax.broadcasted_iota(jnp.int32, (grp, grp), 1)
    causal = ((row // c) == (col // c)) & (row >= col)
    ones_tril = causal.astype(BF16)

    groups = [slice(gi * grp, (gi + 1) * grp) for gi in range(ts // grp)]
    heads = [slice(hh * A_DK, (hh + 1) * A_DK) for hh in range(A_HEADS)]
    chunks = [slice(j * c, (j + 1) * c) for j in range(ts // c)]

    bs = []
    for rows in groups:
        logf = lf_ref[rows, :]
        logf_hi = logf.astype(BF16)
        logf_lo = (logf - logf_hi.astype(F32)).astype(BF16)
        bs.append(_dot(ones_tril, logf_hi) + _dot(ones_tril, logf_lo))
    q_s, k_s = [], []
    for gi, (rows, b) in enumerate(zip(groups, bs)):
        q_s.append((q_ref[rows, :].astype(F32) * jnp.exp(b)).astype(BF16))
        k_sf = k_ref[rows, :].astype(F32) * jnp.exp(-b)
        k_s.append(k_sf.astype(BF16))
        decs = [jnp.exp(b[j * c + c - 1:j * c + c, :]) for j in range(per_grp)]
        dec_rows = jnp.concatenate([jnp.broadcast_to(dj, (c, dj.shape[1])) for dj in decs], axis=0)
        kd_ref[rows, :] = (k_sf * dec_rows).astype(BF16)
        qs_ref[rows, :] = q_s[-1]
        for j, dj in enumerate(decs):
            dec_ref[gi * per_grp + j:gi * per_grp + j + 1, :] = dj

    upd = [[_dot_tn(v_ref[rws, sl], kd_ref[rws, sl]) for rws in chunks] for sl in heads]
    att = [[jnp.where(causal, _dot_nt(q_s[gi][:, sl], k_s[gi][:, sl]), 0.0).astype(BF16) for sl in heads]
           for gi in range(len(groups))]
    for gi, rows in enumerate(groups):
        for hh, sl in enumerate(heads):
            acc_ref[rows, sl] = _dot(att[gi][hh], v_ref[rows, sl])

    seen = []
    for hh, sl in enumerate(heads):
        st = st_ref[hh]
        seen.append([])
        for j in range(len(chunks)):
            seen[hh].append(st.astype(BF16))
            st = st * dec_ref[j:j + 1, sl] + upd[hh][j]
        st_ref[hh] = st

    inter = [[_dot_nt(qs_ref[rws, sl], seen[hh][j]) for j, rws in enumerate(chunks)] for hh, sl in enumerate(heads)]
    og = og_ref[...]
    for hh, sl in enumerate(heads):
        o = acc_ref[:, sl] + jnp.concatenate(inter[hh], axis=0)
        o_ref[:, sl] = (_rmsnorm(o, og) * g_ref[:, sl].astype(F32)).astype(o_ref.dtype)


def _hgrn(y, logf, og, *, batch, seq, ts, interpret):
    ns = seq // ts
    w = A_KW

    def col(cb):
        return pl.BlockSpec((ts, w), lambda b, s: (b * ns + s, cb))

    return pl.pallas_call(
        _hgrn_body,
        grid=(batch, ns),
        in_specs=[col(0), col(1), col(2), col(3), col(0), _resident((1, A_DV))],
        out_specs=col(0),
        out_shape=jax.ShapeDtypeStruct((batch * seq, A_VW), BF16),
        scratch_shapes=[pltpu.VMEM((A_HEADS, A_DV, A_DK), F32), pltpu.VMEM((ts, w), BF16),
                        pltpu.VMEM((ts, w), BF16), pltpu.VMEM((ts, A_VW), F32),
                        pltpu.VMEM((ts // A_CHUNK, w), F32)],
        compiler_params=_params("parallel", "arbitrary"),
        interpret=interpret, name="hgrn2",
    )(y, y, y, y, logf, og)


def _swa_body(sink_ref, q_ref, kc_ref, kp_ref, vc_ref, vp_ref, o_ref, kpad_ref, vext_ref):
    blk = B_BLOCK
    tq = q_ref.shape[0]
    rep = B_Q_HEADS // B_KV_HEADS
    tiles = [(g, pos) for g in range(B_KV_HEADS) for pos in range(2)]

    kk = jnp.concatenate([kp_ref[...], kc_ref[...]], axis=0).astype(F32)
    vv = jnp.concatenate([vp_ref[...], vc_ref[...]], axis=0).astype(F32)
    low = lax.broadcasted_iota(jnp.int32, kk.shape, 1) < B_HEAD_DIM
    kk_sw = pltpu.roll(kk, B_HEAD_DIM, 1)
    vv_sw = pltpu.roll(vv, B_HEAD_DIM, 1)
    for g, pos in tiles:
        keep = low if pos == 0 else ~low
        kpad_ref[2 * g + pos] = jnp.where(keep, kk if g == pos else kk_sw, 0.0).astype(BF16)
        vext_ref[2 * g + pos, :, :B_KW] = jnp.where(keep, vv if g == pos else vv_sw, 0.0).astype(BF16)
        vext_ref[2 * g + pos, :, B_KW:] = jnp.ones((tq + blk, LANES), BF16)

    qi = lax.broadcasted_iota(jnp.int32, (2 * blk, 2 * blk), 0) % blk
    kj = lax.broadcasted_iota(jnp.int32, (2 * blk, 2 * blk), 1)
    diff = qi + blk - kj
    band = (diff >= 0) & (diff < B_WINDOW)
    bias = jnp.where(band, 0.0, -jnp.inf)
    bias_first = jnp.where(band & ((kj + (pl.program_id(1) * tq - blk)) >= 0), 0.0, -jnp.inf)
    upper = lax.broadcasted_iota(jnp.int32, (2 * blk, 1), 0) < blk
    sinks = [jnp.where(upper, sink_ref[rep * g + pos], sink_ref[rep * g + 2 + pos]) * LOG2E for g, pos in tiles]

    for n in range(tq // blk):
        rows = slice(n * blk, (n + 1) * blk)
        keys = slice(n * blk, (n + 2) * blk)
        bn = bias_first if n == 0 else bias
        qs = [jnp.concatenate([q_ref[rows, j * LANES:(j + 1) * LANES] for j in (2 * g, 2 * g + 1)], axis=0)
              for g in range(B_KV_HEADS)]
        s = [_dot_nt(qs[g], kpad_ref[2 * g + pos, keys, :]) + bn for g, pos in tiles]
        m = [jnp.max(si, axis=-1, keepdims=True) for si in s]
        pr = [jnp.exp2(si - mi).astype(BF16) for si, mi in zip(s, m)]
        pv = [_dot(pi, vext_ref[2 * g + pos, keys, :]) for pi, (g, pos) in zip(pr, tiles)]
        res = [pvi[:, :LANES] / (pvi[:, LANES:] + jnp.exp2(sk - mi)) for pvi, sk, mi in zip(pv, sinks, m)]
        for j in range(B_QW // LANES):
            g, i = j // 2, j % 2
            oj = res[2 * g][i * blk:(i + 1) * blk] + res[2 * g + 1][i * blk:(i + 1) * blk]
            o_ref[rows, j * LANES:(j + 1) * LANES] = oj.astype(o_ref.dtype)


def _swa(y, sinks, *, batch, seq, tq, q_cb, k_cb, v_cb, interpret):
    blk = B_BLOCK
    nt = seq // tq
    per = tq // blk

    def cur(width, cb):
        return pl.BlockSpec((tq, width), lambda b, n: (b * nt + n, cb))

    def prev(width, cb):
        return pl.BlockSpec((blk, width), lambda b, n: (jnp.maximum((b * nt + n) * per - 1, 0), cb))

    return pl.pallas_call(
        _swa_body,
        grid=(batch, nt),
        in_specs=[pl.BlockSpec(memory_space=pltpu.SMEM),
                  cur(B_QW, q_cb), cur(B_KW, k_cb), prev(B_KW, k_cb), cur(B_KW, v_cb), prev(B_KW, v_cb)],
        out_specs=cur(B_QW, 0),
        out_shape=jax.ShapeDtypeStruct((batch * seq, B_QW), BF16),
        scratch_shapes=[pltpu.VMEM((4, tq + blk, B_KW), BF16), pltpu.VMEM((4, tq + blk, 2 * LANES), BF16)],
        compiler_params=_params("parallel", "arbitrary"),
        interpret=interpret, name="swa",
    )(sinks, y, y, y, y, y)


def _retention_body(q_ref, k_ref, v_ref, g_ref, og_ref, o_ref, st_ref, decay_ref, rq_ref, rk_ref, *, log_gammas):
    c = C_CHUNK

    @pl.when(pl.program_id(1) == 0)
    def _():
        st_ref[...] = jnp.zeros_like(st_ref)
        row = lax.broadcasted_iota(jnp.int32, (c, c), 0)
        col = lax.broadcasted_iota(jnp.int32, (c, c), 1)
        diff = (row - col).astype(F32)
        rowf = row[:, :LANES].astype(F32)
        for hh, lg in enumerate(log_gammas):
            decay_ref[hh] = jnp.where(diff >= 0, jnp.exp(jnp.maximum(diff, 0.0) * lg), 0.0)
            rq_ref[hh] = jnp.exp((rowf + 1.0) * lg)
            rk_ref[hh] = jnp.exp((c - 1.0 - rowf) * lg)

    nh = len(log_gammas)
    ksl = [slice(hh * C_DK, (hh + 1) * C_DK) for hh in range(nh)]
    vsl = [slice(hh * C_DV, (hh + 1) * C_DV) for hh in range(nh)]
    reps = C_DK // LANES
    att = [(_dot_nt(q_ref[:, ksl[hh]], k_ref[:, ksl[hh]]) * decay_ref[hh]).astype(BF16) for hh in range(nh)]
    q_in = [(q_ref[:, ksl[hh]].astype(F32) * jnp.concatenate([rq_ref[hh]] * reps, axis=1)).astype(BF16)
            for hh in range(nh)]
    k_in = [(k_ref[:, ksl[hh]].astype(F32) * jnp.concatenate([rk_ref[hh]] * reps, axis=1)).astype(BF16)
            for hh in range(nh)]
    o = [_dot(att[hh], v_ref[:, vsl[hh]]) + _dot(q_in[hh], st_ref[hh].astype(BF16)) for hh in range(nh)]
    upd = [_dot_tn(k_in[hh], v_ref[:, vsl[hh]]) for hh in range(nh)]
    for hh, lg in enumerate(log_gammas):
        st_ref[hh] = st_ref[hh] * float(np.exp(c * lg)) + upd[hh]
    og = og_ref[...]
    for hh in range(nh):
        gate = g_ref[:, vsl[hh]].astype(F32)
        o_ref[:, vsl[hh]] = (_rmsnorm(o[hh], og) * gate).astype(o_ref.dtype)


def _retention(y, og, *, batch, seq, interpret):
    c = C_CHUNK
    ns = seq // c
    log_gammas = tuple(float(v) for v in np.log1p(-np.exp2(-5.0 - np.arange(C_HEADS, dtype=np.float64))))

    def blk(width, cb):
        return pl.BlockSpec((c, width), lambda b, s: (b * ns + s, cb))

    return pl.pallas_call(
        functools.partial(_retention_body, log_gammas=log_gammas),
        grid=(batch, ns),
        in_specs=[blk(C_KW, 0), blk(C_KW, 1), blk(C_VW, 1), blk(C_VW, 2), _resident((1, C_DV))],
        out_specs=blk(C_VW, 0),
        out_shape=jax.ShapeDtypeStruct((batch * seq, C_VW), BF16),
        scratch_shapes=[pltpu.VMEM((C_HEADS, C_DK, C_DV), F32), pltpu.VMEM((C_HEADS, c, c), F32),
                        pltpu.VMEM((C_HEADS, c, LANES), F32), pltpu.VMEM((C_HEADS, c, LANES), F32)],
        compiler_params=_params("parallel", "arbitrary"),
        interpret=interpret, name="retention",
    )(y, y, y, y, og)


def _forward(x, p, positions, norm_g, ffn1_w_in, ffn1_w_out, ffn2_w_in, ffn2_w_out, ple_w_proj, ple_w_gate,
             even_w_in, even_w_out, hgrn_lb, hgrn_onorm_g, attn_sinks, odd_w_in, odd_w_out, ret_onorm_g,
             *, interpret=False):
    batch, seq, d = x.shape
    t = batch * seq
    depth = norm_g.shape[0]
    tm = min(512, t)
    kw = dict(interpret=interpret)

    pos = positions.astype(F32).reshape(t, 1)
    inv_b = ROPE_THETA ** (-jnp.arange(0, B_HEAD_DIM, 2, dtype=F32) / B_HEAD_DIM)
    inv_c = RET_THETA ** (-jnp.linspace(0.0, 1.0, C_DK // 2, dtype=F32))
    ang_b = pos * inv_b[None, :]
    ang_c = pos * inv_c[None, :]
    cos_b = jnp.tile(jnp.cos(ang_b), (1, 4))
    sin_b = jnp.tile(jnp.concatenate([-jnp.sin(ang_b), jnp.sin(ang_b)], axis=1), (1, 2))
    cos_c, sin_c = jnp.cos(ang_c), jnp.sin(ang_c)
    lb_all = jnp.cumsum(jax.nn.softmax(hgrn_lb.astype(F32), axis=0), axis=0)
    p_all = p.reshape(depth, t, p.shape[-1])

    h = x.reshape(t, d)
    for i in range(depth):
        gains = norm_g[i]
        j = i // 2
        h = _block(h, gains, ffn1_w_in[i].astype(BF16), ffn1_w_out[i].astype(BF16), ga=0, gb=1, tm=tm, **kw)
        if i % 2 == 0:
            y, logf = _inproj_even(h, gains, even_w_in[j].astype(BF16), lb_all[j][None, :], cos_b, sin_b, tm=tm, **kw)
            oa = _hgrn(y, logf, hgrn_onorm_g[j][None, :], batch=batch, seq=seq, ts=min(512, seq), **kw)
            q_cb = (2 * A_KW + 2 * A_VW) // B_QW
            k_cb = (2 * A_KW + 2 * A_VW + B_QW) // B_KW
            ob = _swa(y, attn_sinks[j].astype(F32), batch=batch, seq=seq, tq=min(512, seq),
                      q_cb=q_cb, k_cb=k_cb, v_cb=k_cb + 1, **kw)
            w_out = even_w_out[j].astype(BF16)
            mix = [(oa, w_out[:A_VW]), (ob, w_out[A_VW:])]
        else:
            y = _inproj_odd(h, gains, odd_w_in[j].astype(BF16), cos_c, sin_c, tm=tm, **kw)
            o = _retention(y, ret_onorm_g[j][None, :], batch=batch, seq=seq, **kw)
            mix = [(o, odd_w_out[j].astype(BF16))]
        h = _block(h, gains, ffn2_w_in[i].astype(BF16), ffn2_w_out[i].astype(BF16), ga=4, gb=5, mix=mix,
                   ple=(p_all, i, ple_w_gate[i].astype(BF16), ple_w_proj[i].astype(BF16)), tm=tm, **kw)
    return h.reshape(batch, seq, d)


def kernel(x, p, positions, norm_g, ffn1_w_in, ffn1_w_out, ffn2_w_in, ffn2_w_out, ple_w_proj, ple_w_gate,
           even_w_in, even_w_out, hgrn_lb, hgrn_onorm_g, attn_sinks, odd_w_in, odd_w_out, ret_onorm_g):
    return _forward(x, p, positions, norm_g, ffn1_w_in, ffn1_w_out, ffn2_w_in, ffn2_w_out, ple_w_proj, ple_w_gate,
                    even_w_in, even_w_out, hgrn_lb, hgrn_onorm_g, attn_sinks, odd_w_in, odd_w_out, ret_onorm_g)
```

```python
import functools

import numpy as np
import jax
import jax.numpy as jnp
from jax import lax
from jax.experimental import pallas as pl
from jax.experimental.pallas import tpu as pltpu

F32 = jnp.float32
BF16 = jnp.bfloat16

EPS = 1e-6
ROPE_THETA = 10000.0
RET_THETA = 10000.0
LOG2E = 1.4426950408889634

A_HEADS, A_DK, A_DV, A_CHUNK = 4, 128, 128, 64
B_Q_HEADS, B_KV_HEADS, B_HEAD_DIM, B_WINDOW, B_BLOCK = 8, 2, 64, 128, 128
C_HEADS, C_DK, C_DV = 4, 256, 512
C_CHUNK = 256

A_KW = A_HEADS * A_DK
A_VW = A_HEADS * A_DV
B_QW = B_Q_HEADS * B_HEAD_DIM
B_KW = B_KV_HEADS * B_HEAD_DIM
C_KW = C_HEADS * C_DK
C_VW = C_HEADS * C_DV

LANES = 128
VMEM_LIMIT_BYTES = 56 * 1024 * 1024

NT_DIMS = (((1,), (1,)), ((), ()))
TN_DIMS = (((0,), (0,)), ((), ()))


def _rmsnorm(x, g):
    return x * lax.rsqrt(jnp.mean(x * x, axis=-1, keepdims=True) + EPS) * g


def _sigmoid(x):
    return 1.0 / (1.0 + jnp.exp(-x))


def _silu(x):
    return x * _sigmoid(x)


def _dot(a, b):
    return jnp.dot(a, b, preferred_element_type=F32)


def _dot_nt(a, b):
    return lax.dot_general(a, b, NT_DIMS, preferred_element_type=F32)


def _dot_tn(a, b):
    return lax.dot_general(a, b, TN_DIMS, preferred_element_type=F32)


def _params(*sem):
    return pltpu.CompilerParams(dimension_semantics=sem, vmem_limit_bytes=VMEM_LIMIT_BYTES)


def _resident(shape):
    return pl.BlockSpec(shape, lambda *_: (0,) * len(shape), pipeline_mode=pl.Buffered(1))


def _rows(tm, width):
    return pl.BlockSpec((tm, width), lambda i: (i, 0))


def _block_body(*refs, n_mix, with_ple, ga, gb, fc, n_sub):
    it = iter(refs)
    h_ref, g_ref = next(it), next(it)
    o_refs = [next(it) for _ in range(n_mix)]
    wo_refs = [next(it) for _ in range(n_mix)]
    win_ref, wout_ref = next(it), next(it)
    if with_ple:
        p_ref, wg_ref, wp_ref = next(it), next(it), next(it)
    out_ref, act_ref = next(it), next(it)
    d_ff = wout_ref.shape[0]
    sub = h_ref.shape[0] // n_sub

    def gain(k):
        return g_ref[k:k + 1, :]

    rows = [slice(s * sub, (s + 1) * sub) for s in range(n_sub)]
    st = [dict() for _ in range(n_sub)]

    def mix_mm(s):
        if n_mix:
            mix = _dot(o_refs[0][rows[s], :], wo_refs[0][...])
            for o_ref, w_ref in zip(o_refs[1:], wo_refs[1:]):
                mix = mix + _dot(o_ref[rows[s], :], w_ref[...])
            st[s]["mix"] = mix

    def pre_norm(s):
        h = h_ref[rows[s], :]
        if n_mix:
            h = h + _rmsnorm(st[s].pop("mix"), gain(3))
        st[s]["h"] = h
        st[s]["x"] = _rmsnorm(h, gain(ga)).astype(BF16)

    def hidden(s):
        x = st[s].pop("x")
        for j in range(d_ff // fc):
            gate = _dot(x, win_ref[:, j * fc:(j + 1) * fc])
            up = _dot(x, win_ref[:, d_ff + j * fc:d_ff + (j + 1) * fc])
            act_ref[rows[s], j * fc:(j + 1) * fc] = (_silu(gate) * up).astype(BF16)

    def down_mm(s):
        st[s]["y"] = _dot(act_ref[rows[s], :], wout_ref[...])

    def post_norm(s):
        st[s]["h"] = st[s]["h"] + 0.5 * _rmsnorm(st[s].pop("y"), gain(gb))

    def ple_mm(s):
        if with_ple:
            st[s]["z"] = _dot(st[s]["h"].astype(BF16), wg_ref[...])
            st[s]["proj"] = _dot(p_ref[rows[s], :].astype(BF16), wp_ref[...])

    def finish(s):
        h = st[s].pop("h")
        if with_ple:
            h = h + _rmsnorm(_sigmoid(st[s].pop("z")) * st[s].pop("proj"), gain(6))
        out_ref[rows[s], :] = h

    a, b = 0, 1
    for stage, s in [(mix_mm, a), (mix_mm, b), (pre_norm, a), (hidden, a), (pre_norm, b), (down_mm, a),
                     (hidden, b), (post_norm, a), (down_mm, b), (ple_mm, a), (post_norm, b), (finish, a),
                     (ple_mm, b), (finish, b)]:
        stage(s)


def _block(h, gains, w_in, w_out, *, ga, gb, mix=(), ple=None, tm, interpret):
    t, d = h.shape
    d_ff = w_out.shape[0]
    os_ = [o for o, _ in mix]
    wos = [w for _, w in mix]
    args = [h, gains, *os_, *wos, w_in, w_out]
    specs = ([_rows(tm, d), _resident(gains.shape)] + [_rows(tm, o.shape[1]) for o in os_]
             + [_resident(w.shape) for w in wos] + [_resident(w_in.shape), _resident(w_out.shape)])
    if ple is not None:
        p_all, layer, w_gate, w_proj = ple
        args += [p_all, w_gate, w_proj]
        specs += [pl.BlockSpec((None, tm, p_all.shape[2]), lambda i: (layer, i, 0)),
                  _resident(w_gate.shape), _resident(w_proj.shape)]
    return pl.pallas_call(
        functools.partial(_block_body, n_mix=len(mix), with_ple=ple is not None, ga=ga, gb=gb, fc=256, n_sub=2),
        grid=(t // tm,),
        in_specs=specs,
        out_specs=_rows(tm, d),
        out_shape=jax.ShapeDtypeStruct((t, d), F32),
        scratch_shapes=[pltpu.VMEM((tm, d_ff), BF16)],
        compiler_params=_params("parallel"),
        interpret=interpret, name="block",
    )(*args)


def _rope64(x, cos, sin_signed):
    n = x.shape[1]
    lane = lax.broadcasted_iota(jnp.int32, x.shape, 1)
    half = B_HEAD_DIM // 2
    swapped = jnp.where((lane % B_HEAD_DIM) < half, pltpu.roll(x, n - half, 1), pltpu.roll(x, half, 1))
    reps = n // cos.shape[1]
    cos_t = jnp.concatenate([cos] * reps, axis=1) if reps > 1 else cos
    sin_t = jnp.concatenate([sin_signed] * reps, axis=1) if reps > 1 else sin_signed
    return x * cos_t + swapped * sin_t


def _inproj_even_body(h_ref, g_ref, w_ref, lb_ref, cos_ref, sin_ref, y_ref, logf_ref):
    x = _rmsnorm(h_ref[...], g_ref[2:3, :]).astype(BF16)
    w = A_KW

    def proj(lo, hi):
        return _dot(x, w_ref[:, lo:hi])

    def put(lo, val):
        y_ref[:, lo:lo + val.shape[1]] = val.astype(y_ref.dtype)

    cos, sin = cos_ref[...], sin_ref[...]
    put(4 * w, _rope64(proj(4 * w, 4 * w + B_QW), cos, sin) * (B_HEAD_DIM ** -0.5 * LOG2E))
    kv = proj(4 * w + B_QW, 4 * w + B_QW + 2 * B_KW)
    put(4 * w + B_QW, _rope64(kv[:, :B_KW], cos, sin))
    put(4 * w + B_QW + B_KW, kv[:, B_KW:])
    lb = lb_ref[...]
    f = lb + (1.0 - lb) * _sigmoid(proj(w, 2 * w))
    logf_ref[...] = jnp.log(f)
    put(w, 1.0 - f)
    put(0, _silu(proj(0, w)))
    put(3 * w, _silu(proj(3 * w, 4 * w)))
    put(2 * w, proj(2 * w, 3 * w))


def _inproj_even(h, gains, w, lb, cos, sin, *, tm, interpret):
    t, d = h.shape
    n = w.shape[1]
    return pl.pallas_call(
        _inproj_even_body,
        grid=(t // tm,),
        in_specs=[_rows(tm, d), _resident(gains.shape), _resident(w.shape), _resident(lb.shape),
                  _rows(tm, LANES), _rows(tm, LANES)],
        out_specs=[_rows(tm, n), _rows(tm, A_KW)],
        out_shape=[jax.ShapeDtypeStruct((t, n), BF16), jax.ShapeDtypeStruct((t, A_KW), F32)],
        compiler_params=_params("parallel"),
        interpret=interpret, name="inproj_even",
    )(h, gains, w, lb, cos, sin)


def _inproj_odd_body(h_ref, g_ref, w_ref, cos_ref, sin_ref, y_ref):
    x = _rmsnorm(h_ref[...], g_ref[2:3, :]).astype(BF16)
    cos, sin = cos_ref[...], sin_ref[...]
    half = C_DK // 2

    def proj(lo, hi):
        return _dot(x, w_ref[:, lo:hi])

    def rope(v):
        x1, x2 = v[:, :half], v[:, half:]
        return jnp.concatenate([x1 * cos - x2 * sin, x2 * cos + x1 * sin], axis=-1)

    for hh in range(C_HEADS):
        lo = hh * C_DK
        y_ref[:, lo:lo + C_DK] = rope(proj(lo, lo + C_DK)).astype(y_ref.dtype)
        lo = C_KW + hh * C_DK
        y_ref[:, lo:lo + C_DK] = (rope(proj(lo, lo + C_DK)) * (C_DK ** -0.5)).astype(y_ref.dtype)
    step = 1024
    for lo in range(2 * C_KW + C_VW, 2 * C_KW + 2 * C_VW, step):
        y_ref[:, lo:lo + step] = _silu(proj(lo, lo + step)).astype(y_ref.dtype)
    for lo in range(2 * C_KW, 2 * C_KW + C_VW, step):
        y_ref[:, lo:lo + step] = proj(lo, lo + step).astype(y_ref.dtype)


def _inproj_odd(h, gains, w, cos, sin, *, tm, interpret):
    t, d = h.shape
    n = w.shape[1]
    return pl.pallas_call(
        _inproj_odd_body,
        grid=(t // tm,),
        in_specs=[_rows(tm, d), _resident(gains.shape), _resident(w.shape), _rows(tm, LANES), _rows(tm, LANES)],
        out_specs=_rows(tm, n),
        out_shape=jax.ShapeDtypeStruct((t, n), BF16),
        compiler_params=_params("parallel"),
        interpret=interpret, name="inproj_odd",
    )(h, gains, w, cos, sin)


def _hgrn_body(q_ref, k_ref, v_ref, g_ref, lf_ref, og_ref, o_ref, st_ref, qs_ref, kd_ref, acc_ref, dec_ref):
    ts = q_ref.shape[0]
    c, grp = A_CHUNK, 2 * LANES
    per_grp = grp // c

    @pl.when(pl.program_id(1) == 0)
    def _():
        st_ref[...] = jnp.zeros_like(st_ref)

    row = lax.broadcasted_iota(jnp.int32, (grp, grp), 0)
    col = lax.broadcasted_iota(jnp.int32, (grp, grp), 1)
    causal = ((row // c) == (col // c)) & (row >= col)
    ones_tril = causal.astype(BF16)

    groups = [slice(gi * grp, (gi + 1) * grp) for gi in range(ts // grp)]
    heads = [slice(hh * A_DK, (hh + 1) * A_DK) for hh in range(A_HEADS)]
    chunks = [slice(j * c, (j + 1) * c) for j in range(ts // c)]

    bs = []
    for rows in groups:
        logf = lf_ref[rows, :]
        logf_hi = logf.astype(BF16)
        logf_lo = (logf - logf_hi.astype(F32)).astype(BF16)
        bs.append(_dot(ones_tril, logf_hi) + _dot(ones_tril, logf_lo))
    q_s, k_s = [], []
    for gi, (rows, b) in enumerate(zip(groups, bs)):
        q_s.append((q_ref[rows, :].astype(F32) * jnp.exp(b)).astype(BF16))
        k_sf = k_ref[rows, :].astype(F32) * jnp.exp(-b)
        k_s.append(k_sf.astype(BF16))
        decs = [jnp.exp(b[j * c + c - 1:j * c + c, :]) for j in range(per_grp)]
        dec_rows = jnp.concatenate([jnp.broadcast_to(dj, (c, dj.shape[1])) for dj in decs], axis=0)
        kd_ref[rows, :] = (k_sf * dec_rows).astype(BF16)
        qs_ref[rows, :] = q_s[-1]
        for j, dj in enumerate(decs):
            dec_ref[gi * per_grp + j:gi * per_grp + j + 1, :] = dj

    upd = [[_dot_tn(v_ref[rws, sl], kd_ref[rws, sl]) for rws in chunks] for sl in heads]
    att = [[jnp.where(causal, _dot_nt(q_s[gi][:, sl], k_s[gi][:, sl]), 0.0).astype(BF16) for sl in heads]
           for gi in range(len(groups))]
    for gi, rows in enumerate(groups):
        for hh, sl in enumerate(heads):
            acc_ref[rows, sl] = _dot(att[gi][hh], v_ref[rows, sl])

    seen = []
    for hh, sl in enumerate(heads):
        st = st_ref[hh]
        seen.append([])
        for j in range(len(chunks)):
            seen[hh].append(st.astype(BF16))
            st = st * dec_ref[j:j + 1, sl] + upd[hh][j]
        st_ref[hh] = st

    inter = [[_dot_nt(qs_ref[rws, sl], seen[hh][j]) for j, rws in enumerate(chunks)] for hh, sl in enumerate(heads)]
    og = og_ref[...]
    for hh, sl in enumerate(heads):
        o = acc_ref[:, sl] + jnp.concatenate(inter[hh], axis=0)
        o_ref[:, sl] = (_rmsnorm(o, og) * g_ref[:, sl].astype(F32)).astype(o_ref.dtype)


def _hgrn(y, logf, og, *, batch, seq, ts, interpret):
    ns = seq // ts
    w = A_KW

    def col(cb):
        return pl.BlockSpec((ts, w), lambda b, s: (b * ns + s, cb))

    return pl.pallas_call(
        _hgrn_body,
        grid=(batch, ns),
        in_specs=[col(0), col(1), col(2), col(3), col(0), _resident((1, A_DV))],
        out_specs=col(0),
        out_shape=jax.ShapeDtypeStruct((batch * seq, A_VW), BF16),
        scratch_shapes=[pltpu.VMEM((A_HEADS, A_DV, A_DK), F32), pltpu.VMEM((ts, w), BF16),
                        pltpu.VMEM((ts, w), BF16), pltpu.VMEM((ts, A_VW), F32),
                        pltpu.VMEM((ts // A_CHUNK, w), F32)],
        compiler_params=_params("parallel", "arbitrary"),
        interpret=interpret, name="hgrn2",
    )(y, y, y, y, logf, og)


def _swa_body(sink_ref, q_ref, kc_ref, kp_ref, vc_ref, vp_ref, o_ref, kpad_ref, vext_ref):
    blk = B_BLOCK
    tq = q_ref.shape[0]
    rep = B_Q_HEADS // B_KV_HEADS
    tiles = [(g, pos) for g in range(B_KV_HEADS) for pos in range(2)]

    kk = jnp.concatenate([kp_ref[...], kc_ref[...]], axis=0).astype(F32)
    vv = jnp.concatenate([vp_ref[...], vc_ref[...]], axis=0).astype(F32)
    low = lax.broadcasted_iota(jnp.int32, kk.shape, 1) < B_HEAD_DIM
    kk_sw = pltpu.roll(kk, B_HEAD_DIM, 1)
    vv_sw = pltpu.roll(vv, B_HEAD_DIM, 1)
    for g, pos in tiles:
        keep = low if pos == 0 else ~low
        kpad_ref[2 * g + pos] = jnp.where(keep, kk if g == pos else kk_sw, 0.0).astype(BF16)
        vext_ref[2 * g + pos, :, :B_KW] = jnp.where(keep, vv if g == pos else vv_sw, 0.0).astype(BF16)
        vext_ref[2 * g + pos, :, B_KW:] = jnp.ones((tq + blk, LANES), BF16)

    qi = lax.broadcasted_iota(jnp.int32, (2 * blk, 2 * blk), 0) % blk
    kj = lax.broadcasted_iota(jnp.int32, (2 * blk, 2 * blk), 1)
    diff = qi + blk - kj
    band = (diff >= 0) & (diff < B_WINDOW)
    bias = jnp.where(band, 0.0, -jnp.inf)
    bias_first = jnp.where(band & ((kj + (pl.program_id(1) * tq - blk)) >= 0), 0.0, -jnp.inf)
    upper = lax.broadcasted_iota(jnp.int32, (2 * blk, 1), 0) < blk
    sinks = [jnp.where(upper, sink_ref[rep * g + pos], sink_ref[rep * g + 2 + pos]) * LOG2E for g, pos in tiles]

    for n in range(tq // blk):
        rows = slice(n * blk, (n + 1) * blk)
        keys = slice(n * blk, (n + 2) * blk)
        bn = bias_first if n == 0 else bias
        qs = [jnp.concatenate([q_ref[rows, j * LANES:(j + 1) * LANES] for j in (2 * g, 2 * g + 1)], axis=0)
              for g in range(B_KV_HEADS)]
        s = [_dot_nt(qs[g], kpad_ref[2 * g + pos, keys, :]) + bn for g, pos in tiles]
        m = [jnp.max(si, axis=-1, keepdims=True) for si in s]
        pr = [jnp.exp2(si - mi).astype(BF16) for si, mi in zip(s, m)]
        pv = [_dot(pi, vext_ref[2 * g + pos, keys, :]) for pi, (g, pos) in zip(pr, tiles)]
        res = [pvi[:, :LANES] / (pvi[:, LANES:] + jnp.exp2(sk - mi)) for pvi, sk, mi in zip(pv, sinks, m)]
        for j in range(B_QW // LANES):
            g, i = j // 2, j % 2
            oj = res[2 * g][i * blk:(i + 1) * blk] + res[2 * g + 1][i * blk:(i + 1) * blk]
            o_ref[rows, j * LANES:(j + 1) * LANES] = oj.astype(o_ref.dtype)


def _swa(y, sinks, *, batch, seq, tq, q_cb, k_cb, v_cb, interpret):
    blk = B_BLOCK
    nt = seq // tq
    per = tq // blk

    def cur(width, cb):
        return pl.BlockSpec((tq, width), lambda b, n: (b * nt + n, cb))

    def prev(width, cb):
        return pl.BlockSpec((blk, width), lambda b, n: (jnp.maximum((b * nt + n) * per - 1, 0), cb))

    return pl.pallas_call(
        _swa_body,
        grid=(batch, nt),
        in_specs=[pl.BlockSpec(memory_space=pltpu.SMEM),
                  cur(B_QW, q_cb), cur(B_KW, k_cb), prev(B_KW, k_cb), cur(B_KW, v_cb), prev(B_KW, v_cb)],
        out_specs=cur(B_QW, 0),
        out_shape=jax.ShapeDtypeStruct((batch * seq, B_QW), BF16),
        scratch_shapes=[pltpu.VMEM((4, tq + blk, B_KW), BF16), pltpu.VMEM((4, tq + blk, 2 * LANES), BF16)],
        compiler_params=_params("parallel", "arbitrary"),
        interpret=interpret, name="swa",
    )(sinks, y, y, y, y, y)


def _retention_body(q_ref, k_ref, v_ref, g_ref, og_ref, o_ref, st_ref, decay_ref, rq_ref, rk_ref, *, log_gammas):
    c = C_CHUNK

    @pl.when(pl.program_id(1) == 0)
    def _():
        st_ref[...] = jnp.zeros_like(st_ref)
        row = lax.broadcasted_iota(jnp.int32, (c, c), 0)
        col = lax.broadcasted_iota(jnp.int32, (c, c), 1)
        diff = (row - col).astype(F32)
        rowf = row[:, :LANES].astype(F32)
        for hh, lg in enumerate(log_gammas):
            decay_ref[hh] = jnp.where(diff >= 0, jnp.exp(jnp.maximum(diff, 0.0) * lg), 0.0)
            rq_ref[hh] = jnp.exp((rowf + 1.0) * lg)
            rk_ref[hh] = jnp.exp((c - 1.0 - rowf) * lg)

    nh = len(log_gammas)
    ksl = [slice(hh * C_DK, (hh + 1) * C_DK) for hh in range(nh)]
    vsl = [slice(hh * C_DV, (hh + 1) * C_DV) for hh in range(nh)]
    reps = C_DK // LANES
    att = [(_dot_nt(q_ref[:, ksl[hh]], k_ref[:, ksl[hh]]) * decay_ref[hh]).astype(BF16) for hh in range(nh)]
    q_in = [(q_ref[:, ksl[hh]].astype(F32) * jnp.concatenate([rq_ref[hh]] * reps, axis=1)).astype(BF16)
            for hh in range(nh)]
    k_in = [(k_ref[:, ksl[hh]].astype(F32) * jnp.concatenate([rk_ref[hh]] * reps, axis=1)).astype(BF16)
            for hh in range(nh)]
    o = [_dot(att[hh], v_ref[:, vsl[hh]]) + _dot(q_in[hh], st_ref[hh].astype(BF16)) for hh in range(nh)]
    upd = [_dot_tn(k_in[hh], v_ref[:, vsl[hh]]) for hh in range(nh)]
    for hh, lg in enumerate(log_gammas):
        st_ref[hh] = st_ref[hh] * float(np.exp(c * lg)) + upd[hh]
    og = og_ref[...]
    for hh in range(nh):
        gate = g_ref[:, vsl[hh]].astype(F32)
        o_ref[:, vsl[hh]] = (_rmsnorm(o[hh], og) * gate).astype(o_ref.dtype)


def _retention(y, og, *, batch, seq, interpret):
    c = C_CHUNK
    ns = seq // c
    log_gammas = tuple(float(v) for v in np.log1p(-np.exp2(-5.0 - np.arange(C_HEADS, dtype=np.float64))))

    def blk(width, cb):
        return pl.BlockSpec((c, width), lambda b, s: (b * ns + s, cb))

    return pl.pallas_call(
        functools.partial(_retention_body, log_gammas=log_gammas),
        grid=(batch, ns),
        in_specs=[blk(C_KW, 0), blk(C_KW, 1), blk(C_VW, 1), blk(C_VW, 2), _resident((1, C_DV))],
        out_specs=blk(C_VW, 0),
        out_shape=jax.ShapeDtypeStruct((batch * seq, C_VW), BF16),
        scratch_shapes=[pltpu.VMEM((C_HEADS, C_DK, C_DV), F32), pltpu.VMEM((C_HEADS, c, c), F32),
                        pltpu.VMEM((C_HEADS, c, LANES), F32), pltpu.VMEM((C_HEADS, c, LANES), F32)],
        compiler_params=_params("parallel", "arbitrary"),
        interpret=interpret, name="retention",
    )(y, y, y, y, og)


def _forward(x, p, positions, norm_g, ffn1_w_in, ffn1_w_out, ffn2_w_in, ffn2_w_out, ple_w_proj, ple_w_gate,
             even_w_in, even_w_out, hgrn_lb, hgrn_onorm_g, attn_sinks, odd_w_in, odd_w_out, ret_onorm_g,
             *, interpret=False):
    batch, seq, d = x.shape
    t = batch * seq
    depth = norm_g.shape[0]
    tm = min(512, t)
    kw = dict(interpret=interpret)

    pos = positions.astype(F32).reshape(t, 1)
    inv_b = ROPE_THETA ** (-jnp.arange(0, B_HEAD_DIM, 2, dtype=F32) / B_HEAD_DIM)
    inv_c = RET_THETA ** (-jnp.linspace(0.0, 1.0, C_DK // 2, dtype=F32))
    ang_b = pos * inv_b[None, :]
    ang_c = pos * inv_c[None, :]
    cos_b = jnp.tile(jnp.cos(ang_b), (1, 4))
    sin_b = jnp.tile(jnp.concatenate([-jnp.sin(ang_b), jnp.sin(ang_b)], axis=1), (1, 2))
    cos_c, sin_c = jnp.cos(ang_c), jnp.sin(ang_c)
    lb_all = jnp.cumsum(jax.nn.softmax(hgrn_lb.astype(F32), axis=0), axis=0)
    p_all = p.reshape(depth, t, p.shape[-1])

    h = x.reshape(t, d)
    for i in range(depth):
        gains = norm_g[i]
        j = i // 2
        h = _block(h, gains, ffn1_w_in[i].astype(BF16), ffn1_w_out[i].astype(BF16), ga=0, gb=1, tm=tm, **kw)
        if i % 2 == 0:
            y, logf = _inproj_even(h, gains, even_w_in[j].astype(BF16), lb_all[j][None, :], cos_b, sin_b, tm=tm, **kw)
            oa = _hgrn(y, logf, hgrn_onorm_g[j][None, :], batch=batch, seq=seq, ts=min(512, seq), **kw)
            q_cb = (2 * A_KW + 2 * A_VW) // B_QW
            k_cb = (2 * A_KW + 2 * A_VW + B_QW) // B_KW
            ob = _swa(y, attn_sinks[j].astype(F32), batch=batch, seq=seq, tq=min(512, seq),
                      q_cb=q_cb, k_cb=k_cb, v_cb=k_cb + 1, **kw)
            w_out = even_w_out[j].astype(BF16)
            mix = [(oa, w_out[:A_VW]), (ob, w_out[A_VW:])]
        else:
            y = _inproj_odd(h, gains, odd_w_in[j].astype(BF16), cos_c, sin_c, tm=tm, **kw)
            o = _retention(y, ret_onorm_g[j][None, :], batch=batch, seq=seq, **kw)
            mix = [(o, odd_w_out[j].astype(BF16))]
        h = _block(h, gains, ffn2_w_in[i].astype(BF16), ffn2_w_out[i].astype(BF16), ga=4, gb=5, mix=mix,
                   ple=(p_all, i, ple_w_gate[i].astype(BF16), ple_w_proj[i].astype(BF16)), tm=tm, **kw)
    return h.reshape(batch, seq, d)


def kernel(x, p, positions, norm_g, ffn1_w_in, ffn1_w_out, ffn2_w_in, ffn2_w_out, ple_w_proj, ple_w_gate,
           even_w_in, even_w_out, hgrn_lb, hgrn_onorm_g, attn_sinks, odd_w_in, odd_w_out, ret_onorm_g):
    return _forward(x, p, positions, norm_g, ffn1_w_in, ffn1_w_out, ffn2_w_in, ffn2_w_out, ple_w_proj, ple_w_gate,
                    even_w_in, even_w_out, hgrn_lb, hgrn_onorm_g, attn_sinks, odd_w_in, odd_w_out, ret_onorm_g)
```

```python
import functools

import numpy as np
import jax
import jax.numpy as jnp
from jax import lax
from jax.experimental import pallas as pl
from jax.experimental.pallas import tpu as pltpu

F32 = jnp.float32
BF16 = jnp.bfloat16

EPS = 1e-6
ROPE_THETA = 10000.0
RET_THETA = 10000.0
LOG2E = 1.4426950408889634

A_HEADS, A_DK, A_DV, A_CHUNK = 4, 128, 128, 64
B_Q_HEADS, B_KV_HEADS, B_HEAD_DIM, B_WINDOW, B_BLOCK = 8, 2, 64, 128, 128
C_HEADS, C_DK, C_DV = 4, 256, 512
C_CHUNK = 256

A_KW = A_HEADS * A_DK
A_VW = A_HEADS * A_DV
B_QW = B_Q_HEADS * B_HEAD_DIM
B_KW = B_KV_HEADS * B_HEAD_DIM
C_KW = C_HEADS * C_DK
C_VW = C_HEADS * C_DV

LANES = 128
VMEM_LIMIT_BYTES = 56 * 1024 * 1024

NT_DIMS = (((1,), (1,)), ((), ()))
TN_DIMS = (((0,), (0,)), ((), ()))


def _rmsnorm(x, g):
    return x * lax.rsqrt(jnp.mean(x * x, axis=-1, keepdims=True) + EPS) * g


def _sigmoid(x):
    return 1.0 / (1.0 + jnp.exp(-x))


def _silu(x):
    return x * _sigmoid(x)


def _dot(a, b):
    return jnp.dot(a, b, preferred_element_type=F32)


def _dot_nt(a, b):
    return lax.dot_general(a, b, NT_DIMS, preferred_element_type=F32)


def _dot_tn(a, b):
    return lax.dot_general(a, b, TN_DIMS, preferred_element_type=F32)


def _params(*sem):
    return pltpu.CompilerParams(dimension_semantics=sem, vmem_limit_bytes=VMEM_LIMIT_BYTES)


def _resident(shape):
    return pl.BlockSpec(shape, lambda *_: (0,) * len(shape), pipeline_mode=pl.Buffered(1))


def _rows(tm, width):
    return pl.BlockSpec((tm, width), lambda i: (i, 0))


def _block_body(*refs, n_mix, with_ple, ga, gb, fc, n_sub):
    it = iter(refs)
    h_ref, g_ref = next(it), next(it)
    o_refs = [next(it) for _ in range(n_mix)]
    wo_refs = [next(it) for _ in range(n_mix)]
    win_ref, wout_ref = next(it), next(it)
    if with_ple:
        p_ref, wg_ref, wp_ref = next(it), next(it), next(it)
    out_ref, act_ref = next(it), next(it)
    d_ff = wout_ref.shape[0]
    sub = h_ref.shape[0] // n_sub

    def gain(k):
        return g_ref[k:k + 1, :]

    rows = [slice(s * sub, (s + 1) * sub) for s in range(n_sub)]
    st = [dict() for _ in range(n_sub)]

    def mix_mm(s):
        if n_mix:
            mix = _dot(o_refs[0][rows[s], :], wo_refs[0][...])
            for o_ref, w_ref in zip(o_refs[1:], wo_refs[1:]):
                mix = mix + _dot(o_ref[rows[s], :], w_ref[...])
            st[s]["mix"] = mix

    def pre_norm(s):
        h = h_ref[rows[s], :]
        if n_mix:
            h = h + _rmsnorm(st[s].pop("mix"), gain(3))
        st[s]["h"] = h
        st[s]["x"] = _rmsnorm(h, gain(ga)).astype(BF16)

    def hidden(s):
        x = st[s].pop("x")
        for j in range(d_ff // fc):
            gate = _dot(x, win_ref[:, j * fc:(j + 1) * fc])
            up = _dot(x, win_ref[:, d_ff + j * fc:d_ff + (j + 1) * fc])
            act_ref[rows[s], j * fc:(j + 1) * fc] = (_silu(gate) * up).astype(BF16)

    def down_mm(s):
        st[s]["y"] = _dot(act_ref[rows[s], :], wout_ref[...])

    def post_norm(s):
        st[s]["h"] = st[s]["h"] + 0.5 * _rmsnorm(st[s].pop("y"), gain(gb))

    def ple_mm(s):
        if with_ple:
            st[s]["z"] = _dot(st[s]["h"].astype(BF16), wg_ref[...])
            st[s]["proj"] = _dot(p_ref[rows[s], :].astype(BF16), wp_ref[...])

    def finish(s):
        h = st[s].pop("h")
        if with_ple:
            h = h + _rmsnorm(_sigmoid(st[s].pop("z")) * st[s].pop("proj"), gain(6))
        out_ref[rows[s], :] = h

    a, b = 0, 1
    for stage, s in [(mix_mm, a), (mix_mm, b), (pre_norm, a), (hidden, a), (pre_norm, b), (down_mm, a),
                     (hidden, b), (post_norm, a), (down_mm, b), (ple_mm, a), (post_norm, b), (finish, a),
                     (ple_mm, b), (finish, b)]:
        stage(s)


def _block(h, gains, w_in, w_out, *, ga, gb, mix=(), ple=None, tm, interpret):
    t, d = h.shape
    d_ff = w_out.shape[0]
    os_ = [o for o, _ in mix]
    wos = [w for _, w in mix]
    args = [h, gains, *os_, *wos, w_in, w_out]
    specs = ([_rows(tm, d), _resident(gains.shape)] + [_rows(tm, o.shape[1]) for o in os_]
             + [_resident(w.shape) for w in wos] + [_resident(w_in.shape), _resident(w_out.shape)])
    if ple is not None:
        p_all, layer, w_gate, w_proj = ple
        args += [p_all, w_gate, w_proj]
        specs += [pl.BlockSpec((None, tm, p_all.shape[2]), lambda i: (layer, i, 0)),
                  _resident(w_gate.shape), _resident(w_proj.shape)]
    return pl.pallas_call(
        functools.partial(_block_body, n_mix=len(mix), with_ple=ple is not None, ga=ga, gb=gb, fc=256, n_sub=2),
        grid=(t // tm,),
        in_specs=specs,
        out_specs=_rows(tm, d),
        out_shape=jax.ShapeDtypeStruct((t, d), F32),
        scratch_shapes=[pltpu.VMEM((tm, d_ff), BF16)],
        compiler_params=_params("parallel"),
        interpret=interpret, name="block",
    )(*args)


def _rope64(x, cos, sin_signed):
    n = x.shape[1]
    lane = lax.broadcasted_iota(jnp.int32, x.shape, 1)
    half = B_HEAD_DIM // 2
    swapped = jnp.where((lane % B_HEAD_DIM) < half, pltpu.roll(x, n - half, 1), pltpu.roll(x, half, 1))
    reps = n // cos.shape[1]
    cos_t = jnp.concatenate([cos] * reps, axis=1) if reps > 1 else cos
    sin_t = jnp.concatenate([sin_signed] * reps, axis=1) if reps > 1 else sin_signed
    return x * cos_t + swapped * sin_t


def _inproj_even_body(h_ref, g_ref, w_ref, lb_ref, cos_ref, sin_ref, y_ref, logf_ref):
    x = _rmsnorm(h_ref[...], g_ref[2:3, :]).astype(BF16)
    w = A_KW

    def proj(lo, hi):
        return _dot(x, w_ref[:, lo:hi])

    def put(lo, val):
        y_ref[:, lo:lo + val.shape[1]] = val.astype(y_ref.dtype)

    cos, sin = cos_ref[...], sin_ref[...]
    put(4 * w, _rope64(proj(4 * w, 4 * w + B_QW), cos, sin) * (B_HEAD_DIM ** -0.5 * LOG2E))
    kv = proj(4 * w + B_QW, 4 * w + B_QW + 2 * B_KW)
    put(4 * w + B_QW, _rope64(kv[:, :B_KW], cos, sin))
    put(4 * w + B_QW + B_KW, kv[:, B_KW:])
    lb = lb_ref[...]
    f = lb + (1.0 - lb) * _sigmoid(proj(w, 2 * w))
    logf_ref[...] = jnp.log(f)
    put(w, 1.0 - f)
    put(0, _silu(proj(0, w)))
    put(3 * w, _silu(proj(3 * w, 4 * w)))
    put(2 * w, proj(2 * w, 3 * w))


def _inproj_even(h, gains, w, lb, cos, sin, *, tm, interpret):
    t, d = h.shape
    n = w.shape[1]
    return pl.pallas_call(
        _inproj_even_body,
        grid=(t // tm,),
        in_specs=[_rows(tm, d), _resident(gains.shape), _resident(w.shape), _resident(lb.shape),
                  _rows(tm, LANES), _rows(tm, LANES)],
        out_specs=[_rows(tm, n), _rows(tm, A_KW)],
        out_shape=[jax.ShapeDtypeStruct((t, n), BF16), jax.ShapeDtypeStruct((t, A_KW), F32)],
        compiler_params=_params("parallel"),
        interpret=interpret, name="inproj_even",
    )(h, gains, w, lb, cos, sin)


def _inproj_odd_body(h_ref, g_ref, w_ref, cos_ref, sin_ref, y_ref):
    x = _rmsnorm(h_ref[...], g_ref[2:3, :]).astype(BF16)
    cos, sin = cos_ref[...], sin_ref[...]
    half = C_DK // 2

    def proj(lo, hi):
        return _dot(x, w_ref[:, lo:hi])

    def rope(v):
        x1, x2 = v[:, :half], v[:, half:]
        return jnp.concatenate([x1 * cos - x2 * sin, x2 * cos + x1 * sin], axis=-1)

    for hh in range(C_HEADS):
        lo = hh * C_DK
        y_ref[:, lo:lo + C_DK] = rope(proj(lo, lo + C_DK)).astype(y_ref.dtype)
        lo = C_KW + hh * C_DK
        y_ref[:, lo:lo + C_DK] = (rope(proj(lo, lo + C_DK)) * (C_DK ** -0.5)).astype(y_ref.dtype)
    step = 1024
    for lo in range(2 * C_KW + C_VW, 2 * C_KW + 2 * C_VW, step):
        y_ref[:, lo:lo + step] = _silu(proj(lo, lo + step)).astype(y_ref.dtype)
    for lo in range(2 * C_KW, 2 * C_KW + C_VW, step):
        y_ref[:, lo:lo + step] = proj(lo, lo + step).astype(y_ref.dtype)


def _inproj_odd(h, gains, w, cos, sin, *, tm, interpret):
    t, d = h.shape
    n = w.shape[1]
    return pl.pallas_call(
        _inproj_odd_body,
        grid=(t // tm,),
        in_specs=[_rows(tm, d), _resident(gains.shape), _resident(w.shape), _rows(tm, LANES), _rows(tm, LANES)],
        out_specs=_rows(tm, n),
        out_shape=jax.ShapeDtypeStruct((t, n), BF16),
        compiler_params=_params("parallel"),
        interpret=interpret, name="inproj_odd",
    )(h, gains, w, cos, sin)


def _hgrn_body(q_ref, k_ref, v_ref, g_ref, lf_ref, og_ref, o_ref, st_ref, qs_ref, kd_ref, acc_ref, dec_ref):
    ts = q_ref.shape[0]
    c, grp = A_CHUNK, 2 * LANES
    per_grp = grp // c

    @pl.when(pl.program_id(1) == 0)
    def _():
        st_ref[...] = jnp.zeros_like(st_ref)

    row = lax.broadcasted_iota(jnp.int32, (grp, grp), 0)
    col = lax.broadcasted_iota(jnp.int32, (grp, grp), 1)
    causal = ((row // c) == (col // c)) & (row >= col)
    ones_tril = causal.astype(BF16)

    groups = [slice(gi * grp, (gi + 1) * grp) for gi in range(ts // grp)]
    heads = [slice(hh * A_DK, (hh + 1) * A_DK) for hh in range(A_HEADS)]
    chunks = [slice(j * c, (j + 1) * c) for j in range(ts // c)]

    bs = []
    for rows in groups:
        logf = lf_ref[rows, :]
        logf_hi = logf.astype(BF16)
        logf_lo = (logf - logf_hi.astype(F32)).astype(BF16)
        bs.append(_dot(ones_tril, logf_hi) + _dot(ones_tril, logf_lo))
    q_s, k_s = [], []
    for gi, (rows, b) in enumerate(zip(groups, bs)):
        q_s.append((q_ref[rows, :].astype(F32) * jnp.exp(b)).astype(BF16))
        k_sf = k_ref[rows, :].astype(F32) * jnp.exp(-b)
        k_s.append(k_sf.astype(BF16))
        decs = [jnp.exp(b[j * c + c - 1:j * c + c, :]) for j in range(per_grp)]
        dec_rows = jnp.concatenate([jnp.broadcast_to(dj, (c, dj.shape[1])) for dj in decs], axis=0)
        kd_ref[rows, :] = (k_sf * dec_rows).astype(BF16)
        qs_ref[rows, :] = q_s[-1]
        for j, dj in enumerate(decs):
            dec_ref[gi * per_grp + j:gi * per_grp + j + 1, :] = dj

    upd = [[_dot_tn(v_ref[rws, sl], kd_ref[rws, sl]) for rws in chunks] for sl in heads]
    att = [[jnp.where(causal, _dot_nt(q_s[gi][:, sl], k_s[gi][:, sl]), 0.0).astype(BF16) for sl in heads]
           for gi in range(len(groups))]
    for gi, rows in enumerate(groups):
        for hh, sl in enumerate(heads):
            acc_ref[rows, sl] = _dot(att[gi][hh], v_ref[rows, sl])

    seen = []
    for hh, sl in enumerate(heads):
        st = st_ref[hh]
        seen.append([])
        for j in range(len(chunks)):
            seen[hh].append(st.astype(BF16))
            st = st * dec_ref[j:j + 1, sl] + upd[hh][j]
        st_ref[hh] = st

    inter = [[_dot_nt(qs_ref[rws, sl], seen[hh][j]) for j, rws in enumerate(chunks)] for hh, sl in enumerate(heads)]
    og = og_ref[...]
    for hh, sl in enumerate(heads):
        o = acc_ref[:, sl] + jnp.concatenate(inter[hh], axis=0)
        o_ref[:, sl] = (_rmsnorm(o, og) * g_ref[:, sl].astype(F32)).astype(o_ref.dtype)


def _hgrn(y, logf, og, *, batch, seq, ts, interpret):
    ns = seq // ts
    w = A_KW

    def col(cb):
        return pl.BlockSpec((ts, w), lambda b, s: (b * ns + s, cb))

    return pl.pallas_call(
        _hgrn_body,
        grid=(batch, ns),
        in_specs=[col(0), col(1), col(2), col(3), col(0), _resident((1, A_DV))],
        out_specs=col(0),
        out_shape=jax.ShapeDtypeStruct((batch * seq, A_VW), BF16),
        scratch_shapes=[pltpu.VMEM((A_HEADS, A_DV, A_DK), F32), pltpu.VMEM((ts, w), BF16),
                        pltpu.VMEM((ts, w), BF16), pltpu.VMEM((ts, A_VW), F32),
                        pltpu.VMEM((ts // A_CHUNK, w), F32)],
        compiler_params=_params("parallel", "arbitrary"),
        interpret=interpret, name="hgrn2",
    )(y, y, y, y, logf, og)


def _swa_body(sink_ref, q_ref, kc_ref, kp_ref, vc_ref, vp_ref, o_ref, kpad_ref, vext_ref):
    blk = B_BLOCK
    tq = q_ref.shape[0]
    rep = B_Q_HEADS // B_KV_HEADS
    tiles = [(g, pos) for g in range(B_KV_HEADS) for pos in range(2)]

    kk = jnp.concatenate([kp_ref[...], kc_ref[...]], axis=0).astype(F32)
    vv = jnp.concatenate([vp_ref[...], vc_ref[...]], axis=0).astype(F32)
    low = lax.broadcasted_iota(jnp.int32, kk.shape, 1) < B_HEAD_DIM
    kk_sw = pltpu.roll(kk, B_HEAD_DIM, 1)
    vv_sw = pltpu.roll(vv, B_HEAD_DIM, 1)
    for g, pos in tiles:
        keep = low if pos == 0 else ~low
        kpad_ref[2 * g + pos] = jnp.where(keep, kk if g == pos else kk_sw, 0.0).astype(BF16)
        vext_ref[2 * g + pos, :, :B_KW] = jnp.where(keep, vv if g == pos else vv_sw, 0.0).astype(BF16)
        vext_ref[2 * g + pos, :, B_KW:] = jnp.ones((tq + blk, LANES), BF16)

    qi = lax.broadcasted_iota(jnp.int32, (2 * blk, 2 * blk), 0) % blk
    kj = lax.broadcasted_iota(jnp.int32, (2 * blk, 2 * blk), 1)
    diff = qi + blk - kj
    band = (diff >= 0) & (diff < B_WINDOW)
    bias = jnp.where(band, 0.0, -jnp.inf)
    bias_first = jnp.where(band & ((kj + (pl.program_id(1) * tq - blk)) >= 0), 0.0, -jnp.inf)
    upper = lax.broadcasted_iota(jnp.int32, (2 * blk, 1), 0) < blk
    sinks = [jnp.where(upper, sink_ref[rep * g + pos], sink_ref[rep * g + 2 + pos]) * LOG2E for g, pos in tiles]

    for n in range(tq // blk):
        rows = slice(n * blk, (n + 1) * blk)
        keys = slice(n * blk, (n + 2) * blk)
        bn = bias_first if n == 0 else bias
        qs = [jnp.concatenate([q_ref[rows, j * LANES:(j + 1) * LANES] for j in (2 * g, 2 * g + 1)], axis=0)
              for g in range(B_KV_HEADS)]
        s = [_dot_nt(qs[g], kpad_ref[2 * g + pos, keys, :]) + bn for g, pos in tiles]
        m = [jnp.max(si, axis=-1, keepdims=True) for si in s]
        pr = [jnp.exp2(si - mi).astype(BF16) for si, mi in zip(s, m)]
        pv = [_dot(pi, vext_ref[2 * g + pos, keys, :]) for pi, (g, pos) in zip(pr, tiles)]
        res = [pvi[:, :LANES] / (pvi[:, LANES:] + jnp.exp2(sk - mi)) for pvi, sk, mi in zip(pv, sinks, m)]
        for j in range(B_QW // LANES):
            g, i = j // 2, j % 2
            oj = res[2 * g][i * blk:(i + 1) * blk] + res[2 * g + 1][i * blk:(i + 1) * blk]
            o_ref[rows, j * LANES:(j + 1) * LANES] = oj.astype(o_ref.dtype)


def _swa(y, sinks, *, batch, seq, tq, q_cb, k_cb, v_cb, interpret):
    blk = B_BLOCK
    nt = seq // tq
    per = tq // blk

    def cur(width, cb):
        return pl.BlockSpec((tq, width), lambda b, n: (b * nt + n, cb))

    def prev(width, cb):
        return pl.BlockSpec((blk, width), lambda b, n: (jnp.maximum((b * nt + n) * per - 1, 0), cb))

    return pl.pallas_call(
        _swa_body,
        grid=(batch, nt),
        in_specs=[pl.BlockSpec(memory_space=pltpu.SMEM),
                  cur(B_QW, q_cb), cur(B_KW, k_cb), prev(B_KW, k_cb), cur(B_KW, v_cb), prev(B_KW, v_cb)],
        out_specs=cur(B_QW, 0),
        out_shape=jax.ShapeDtypeStruct((batch * seq, B_QW), BF16),
        scratch_shapes=[pltpu.VMEM((4, tq + blk, B_KW), BF16), pltpu.VMEM((4, tq + blk, 2 * LANES), BF16)],
        compiler_params=_params("parallel", "arbitrary"),
        interpret=interpret, name="swa",
    )(sinks, y, y, y, y, y)


def _retention_body(q_ref, k_ref, v_ref, g_ref, og_ref, o_ref, st_ref, decay_ref, rq_ref, rk_ref, *, log_gammas):
    c = C_CHUNK

    @pl.when(pl.program_id(1) == 0)
    def _():
        st_ref[...] = jnp.zeros_like(st_ref)
        row = lax.broadcasted_iota(jnp.int32, (c, c), 0)
        col = lax.broadcasted_iota(jnp.int32, (c, c), 1)
        diff = (row - col).astype(F32)
        rowf = row[:, :LANES].astype(F32)
        for hh, lg in enumerate(log_gammas):
            decay_ref[hh] = jnp.where(diff >= 0, jnp.exp(jnp.maximum(diff, 0.0) * lg), 0.0)
            rq_ref[hh] = jnp.exp((rowf + 1.0) * lg)
            rk_ref[hh] = jnp.exp((c - 1.0 - rowf) * lg)

    nh = len(log_gammas)
    ksl = [slice(hh * C_DK, (hh + 1) * C_DK) for hh in range(nh)]
    vsl = [slice(hh * C_DV, (hh + 1) * C_DV) for hh in range(nh)]
    reps = C_DK // LANES
    att = [(_dot_nt(q_ref[:, ksl[hh]], k_ref[:, ksl[hh]]) * decay_ref[hh]).astype(BF16) for hh in range(nh)]
    q_in = [(q_ref[:, ksl[hh]].astype(F32) * jnp.concatenate([rq_ref[hh]] * reps, axis=1)).astype(BF16)
            for hh in range(nh)]
    k_in = [(k_ref[:, ksl[hh]].astype(F32) * jnp.concatenate([rk_ref[hh]] * reps, axis=1)).astype(BF16)
            for hh in range(nh)]
    o = [_dot(att[hh], v_ref[:, vsl[hh]]) + _dot(q_in[hh], st_ref[hh].astype(BF16)) for hh in range(nh)]
    upd = [_dot_tn(k_in[hh], v_ref[:, vsl[hh]]) for hh in range(nh)]
    for hh, lg in enumerate(log_gammas):
        st_ref[hh] = st_ref[hh] * float(np.exp(c * lg)) + upd[hh]
    og = og_ref[...]
    for hh in range(nh):
        gate = g_ref[:, vsl[hh]].astype(F32)
        o_ref[:, vsl[hh]] = (_rmsnorm(o[hh], og) * gate).astype(o_ref.dtype)


def _retention(y, og, *, batch, seq, interpret):
    c = C_CHUNK
    ns = seq // c
    log_gammas = tuple(float(v) for v in np.log1p(-np.exp2(-5.0 - np.arange(C_HEADS, dtype=np.float64))))

    def blk(width, cb):
        return pl.BlockSpec((c, width), lambda b, s: (b * ns + s, cb))

    return pl.pallas_call(
        functools.partial(_retention_body, log_gammas=log_gammas),
        grid=(batch, ns),
        in_specs=[blk(C_KW, 0), blk(C_KW, 1), blk(C_VW, 1), blk(C_VW, 2), _resident((1, C_DV))],
        out_specs=blk(C_VW, 0),
        out_shape=jax.ShapeDtypeStruct((batch * seq, C_VW), BF16),
        scratch_shapes=[pltpu.VMEM((C_HEADS, C_DK, C_DV), F32), pltpu.VMEM((C_HEADS, c, c), F32),
                        pltpu.VMEM((C_HEADS, c, LANES), F32), pltpu.VMEM((C_HEADS, c, LANES), F32)],
        compiler_params=_params("parallel", "arbitrary"),
        interpret=interpret, name="retention",
    )(y, y, y, y, og)


def _odd_mixer_body(h_ref, g_ref, w_ref, cos_ref, sin_ref, og_ref, o_ref, y_scr, st_ref, decay_ref, rq_ref, rk_ref,
                    *, log_gammas, tiles_per_seq):
    c = C_CHUNK
    half = C_DK // 2
    nh = len(log_gammas)
    i = pl.program_id(0)

    @pl.when(i == 0)
    def _():
        y_scr[...] = jnp.zeros_like(y_scr)
        st_ref[...] = jnp.zeros_like(st_ref)
        row = lax.broadcasted_iota(jnp.int32, (c, c), 0)
        col = lax.broadcasted_iota(jnp.int32, (c, c), 1)
        diff = (row - col).astype(F32)
        rowf = row[:, :LANES].astype(F32)
        for hh, lg in enumerate(log_gammas):
            decay_ref[hh] = jnp.where(diff >= 0, jnp.exp(jnp.maximum(diff, 0.0) * lg), 0.0)
            rq_ref[hh] = jnp.exp((rowf + 1.0) * lg)
            rk_ref[hh] = jnp.exp((c - 1.0 - rowf) * lg)

    @pl.when(lax.rem(i + tiles_per_seq - 1, tiles_per_seq) == 0)
    def _():
        st_ref[...] = jnp.zeros_like(st_ref)

    y_new = y_scr.at[lax.rem(i, 2)]
    y_old = y_scr.at[lax.rem(i + 1, 2)]
    ksl = [slice(hh * C_DK, (hh + 1) * C_DK) for hh in range(nh)]
    vsl = [slice(hh * C_DV, (hh + 1) * C_DV) for hh in range(nh)]
    k0, v0, g0 = C_KW, 2 * C_KW, 2 * C_KW + C_VW
    reps = C_DK // LANES
    cos, sin = cos_ref[...], sin_ref[...]
    r = {}

    def rope(v):
        x1, x2 = v[:, :half], v[:, half:]
        return jnp.concatenate([x1 * cos - x2 * sin, x2 * cos + x1 * sin], axis=-1)

    def proj(lo, hi):
        return _dot(r["x"], w_ref[:, lo:hi])

    def p_norm():
        r["x"] = _rmsnorm(h_ref[...], g_ref[2:3, :]).astype(BF16)

    def p_q():
        for sl in ksl:
            y_new[:, sl] = rope(proj(sl.start, sl.stop)).astype(BF16)

    def p_k():
        for sl in ksl:
            y_new[:, k0 + sl.start:k0 + sl.stop] = (rope(proj(k0 + sl.start, k0 + sl.stop))
                                                      * (C_DK ** -0.5)).astype(BF16)

    def p_gate(j):
        lo = g0 + j * (C_VW // 2)
        y_new[:, lo:lo + C_VW // 2] = _silu(proj(lo, lo + C_VW // 2)).astype(BF16)

    def p_value(j):
        lo = v0 + j * (C_VW // 2)
        y_new[:, lo:lo + C_VW // 2] = proj(lo, lo + C_VW // 2).astype(BF16)

    def r_att():
        r["att"] = [(_dot_nt(y_old[:, sl], y_old[:, k0 + sl.start:k0 + sl.stop]) * decay_ref[hh]).astype(BF16)
                    for hh, sl in enumerate(ksl)]

    def r_scale():
        r["q_in"] = [(y_old[:, sl].astype(F32) * jnp.concatenate([rq_ref[hh]] * reps, axis=1)).astype(BF16)
                     for hh, sl in enumerate(ksl)]
        r["k_in"] = [(y_old[:, k0 + sl.start:k0 + sl.stop].astype(F32)
                      * jnp.concatenate([rk_ref[hh]] * reps, axis=1)).astype(BF16) for hh, sl in enumerate(ksl)]

    def r_mm():
        vs = [y_old[:, v0 + sl.start:v0 + sl.stop] for sl in vsl]
        r["o"] = [_dot(r["att"][hh], vs[hh]) + _dot(r["q_in"][hh], st_ref[hh].astype(BF16)) for hh in range(nh)]
        r["upd"] = [_dot_tn(r["k_in"][hh], vs[hh]) for hh in range(nh)]

    def r_state():
        for hh, lg in enumerate(log_gammas):
            st_ref[hh] = st_ref[hh] * float(np.exp(c * lg)) + r["upd"][hh]

    def r_out():
        og = og_ref[...]
        for hh, sl in enumerate(vsl):
            gate = y_old[:, g0 + sl.start:g0 + sl.stop].astype(F32)
            o_ref[:, sl] = (_rmsnorm(r["o"][hh], og) * gate).astype(o_ref.dtype)

    for stage in (r_att, p_norm, p_q, r_scale, p_k, r_mm, lambda: p_gate(0), r_state, lambda: p_gate(1), r_out,
                  lambda: p_value(0), lambda: p_value(1)):
        stage()


def _odd_mixer(h, gains, w, cos, sin, og, *, seq, interpret):
    t, d = h.shape
    c = C_CHUNK
    n = t // c
    log_gammas = tuple(float(v) for v in np.log1p(-np.exp2(-5.0 - np.arange(C_HEADS, dtype=np.float64))))

    def cur(width):
        return pl.BlockSpec((c, width), lambda i: (jnp.minimum(i, n - 1), 0))

    return pl.pallas_call(
        functools.partial(_odd_mixer_body, log_gammas=log_gammas, tiles_per_seq=seq // c),
        grid=(n + 1,),
        in_specs=[cur(d), _resident(gains.shape), _resident(w.shape), cur(LANES), cur(LANES), _resident(og.shape)],
        out_specs=pl.BlockSpec((c, C_VW), lambda i: (jnp.maximum(i - 1, 0), 0)),
        out_shape=jax.ShapeDtypeStruct((t, C_VW), BF16),
        scratch_shapes=[pltpu.VMEM((2, c, w.shape[1]), BF16), pltpu.VMEM((C_HEADS, C_DK, C_DV), F32),
                        pltpu.VMEM((C_HEADS, c, c), F32), pltpu.VMEM((C_HEADS, c, LANES), F32),
                        pltpu.VMEM((C_HEADS, c, LANES), F32)],
        compiler_params=_params("arbitrary"),
        interpret=interpret, name="odd_mixer",
    )(h, gains, w, cos, sin, og)


def _forward(x, p, positions, norm_g, ffn1_w_in, ffn1_w_out, ffn2_w_in, ffn2_w_out, ple_w_proj, ple_w_gate,
             even_w_in, even_w_out, hgrn_lb, hgrn_onorm_g, attn_sinks, odd_w_in, odd_w_out, ret_onorm_g,
             *, interpret=False):
    batch, seq, d = x.shape
    t = batch * seq
    depth = norm_g.shape[0]
    tm = min(512, t)
    kw = dict(interpret=interpret)

    pos = positions.astype(F32).reshape(t, 1)
    inv_b = ROPE_THETA ** (-jnp.arange(0, B_HEAD_DIM, 2, dtype=F32) / B_HEAD_DIM)
    inv_c = RET_THETA ** (-jnp.linspace(0.0, 1.0, C_DK // 2, dtype=F32))
    ang_b = pos * inv_b[None, :]
    ang_c = pos * inv_c[None, :]
    cos_b = jnp.tile(jnp.cos(ang_b), (1, 4))
    sin_b = jnp.tile(jnp.concatenate([-jnp.sin(ang_b), jnp.sin(ang_b)], axis=1), (1, 2))
    cos_c, sin_c = jnp.cos(ang_c), jnp.sin(ang_c)
    lb_all = jnp.cumsum(jax.nn.softmax(hgrn_lb.astype(F32), axis=0), axis=0)
    p_all = p.reshape(depth, t, p.shape[-1])

    h = x.reshape(t, d)
    for i in range(depth):
        gains = norm_g[i]
        j = i // 2
        h = _block(h, gains, ffn1_w_in[i].astype(BF16), ffn1_w_out[i].astype(BF16), ga=0, gb=1, tm=tm, **kw)
        if i % 2 == 0:
            y, logf = _inproj_even(h, gains, even_w_in[j].astype(BF16), lb_all[j][None, :], cos_b, sin_b, tm=tm, **kw)
            oa = _hgrn(y, logf, hgrn_onorm_g[j][None, :], batch=batch, seq=seq, ts=min(512, seq), **kw)
            q_cb = (2 * A_KW + 2 * A_VW) // B_QW
            k_cb = (2 * A_KW + 2 * A_VW + B_QW) // B_KW
            ob = _swa(y, attn_sinks[j].astype(F32), batch=batch, seq=seq, tq=min(512, seq),
                      q_cb=q_cb, k_cb=k_cb, v_cb=k_cb + 1, **kw)
            w_out = even_w_out[j].astype(BF16)
            mix = [(oa, w_out[:A_VW]), (ob, w_out[A_VW:])]
        else:
            o = _odd_mixer(h, gains, odd_w_in[j].astype(BF16), cos_c, sin_c, ret_onorm_g[j][None, :], seq=seq, **kw)
            mix = [(o, odd_w_out[j].astype(BF16))]
        h = _block(h, gains, ffn2_w_in[i].astype(BF16), ffn2_w_out[i].astype(BF16), ga=4, gb=5, mix=mix,
                   ple=(p_all, i, ple_w_gate[i].astype(BF16), ple_w_proj[i].astype(BF16)), tm=tm, **kw)
    return h.reshape(batch, seq, d)


def kernel(x, p, positions, norm_g, ffn1_w_in, ffn1_w_out, ffn2_w_in, ffn2_w_out, ple_w_proj, ple_w_gate,
           even_w_in, even_w_out, hgrn_lb, hgrn_onorm_g, attn_sinks, odd_w_in, odd_w_out, ret_onorm_g):
    return _forward(x, p, positions, norm_g, ffn1_w_in, ffn1_w_out, ffn2_w_in, ffn2_w_out, ple_w_proj, ple_w_gate,
                    even_w_in, even_w_out, hgrn_lb, hgrn_onorm_g, attn_sinks, odd_w_in, odd_w_out, ret_onorm_g)
```

```python
import functools

import numpy as np
import jax
import jax.numpy as jnp
from jax import lax
from jax.experimental import pallas as pl
from jax.experimental.pallas import tpu as pltpu

F32 = jnp.float32
BF16 = jnp.bfloat16

EPS = 1e-6
ROPE_THETA = 10000.0
RET_THETA = 10000.0
LOG2E = 1.4426950408889634

A_HEADS, A_DK, A_DV, A_CHUNK = 4, 128, 128, 64
B_Q_HEADS, B_KV_HEADS, B_HEAD_DIM, B_WINDOW, B_BLOCK = 8, 2, 64, 128, 128
C_HEADS, C_DK, C_DV = 4, 256, 512
C_CHUNK = 256

A_KW = A_HEADS * A_DK
A_VW = A_HEADS * A_DV
B_QW = B_Q_HEADS * B_HEAD_DIM
B_KW = B_KV_HEADS * B_HEAD_DIM
C_KW = C_HEADS * C_DK
C_VW = C_HEADS * C_DV

LANES = 128
VMEM_LIMIT_BYTES = 56 * 1024 * 1024

NT_DIMS = (((1,), (1,)), ((), ()))
TN_DIMS = (((0,), (0,)), ((), ()))


def _rmsnorm(x, g):
    return x * lax.rsqrt(jnp.mean(x * x, axis=-1, keepdims=True) + EPS) * g


def _sigmoid(x):
    return 0.5 * jnp.tanh(0.5 * x) + 0.5


def _silu(x):
    hx = 0.5 * x
    return hx + hx * jnp.tanh(hx)


def _dot(a, b):
    return jnp.dot(a, b, preferred_element_type=F32)


def _dot_nt(a, b):
    return lax.dot_general(a, b, NT_DIMS, preferred_element_type=F32)


def _dot_tn(a, b):
    return lax.dot_general(a, b, TN_DIMS, preferred_element_type=F32)


def _params(*sem):
    return pltpu.CompilerParams(dimension_semantics=sem, vmem_limit_bytes=VMEM_LIMIT_BYTES)


def _resident(shape):
    return pl.BlockSpec(shape, lambda *_: (0,) * len(shape), pipeline_mode=pl.Buffered(1))


def _layer(stacked, layer):
    shape = stacked.shape[1:]
    return pl.BlockSpec((None,) + shape, lambda *_: (layer,) + (0,) * len(shape), pipeline_mode=pl.Buffered(1))


def _rows(tm, width):
    return pl.BlockSpec((tm, width), lambda i: (i, 0))


def _block_body(*refs, n_mix, with_ple, ga, gb, fc, n_sub):
    it = iter(refs)
    h_ref, g_ref = next(it), next(it)
    o_refs = [next(it) for _ in range(n_mix)]
    wo_ref = next(it) if n_mix else None
    win_ref, wout_ref = next(it), next(it)
    if with_ple:
        p_ref, wg_ref, wp_ref = next(it), next(it), next(it)
    out_ref, act_ref = next(it), next(it)
    d_ff = wout_ref.shape[0]
    sub = h_ref.shape[0] // n_sub

    def gain(k):
        return g_ref[k:k + 1, :]

    rows = [slice(s * sub, (s + 1) * sub) for s in range(n_sub)]
    st = [dict() for _ in range(n_sub)]

    def mix_mm(s):
        if n_mix:
            mix, lo = None, 0
            for o_ref in o_refs:
                part = _dot(o_ref[rows[s], :], wo_ref[lo:lo + o_ref.shape[1], :])
                mix = part if mix is None else mix + part
                lo += o_ref.shape[1]
            st[s]["mix"] = mix

    def pre_norm(s):
        h = h_ref[rows[s], :]
        if n_mix:
            h = h + _rmsnorm(st[s].pop("mix"), gain(3))
        st[s]["h"] = h
        st[s]["x"] = _rmsnorm(h, gain(ga)).astype(BF16)

    def hidden(s):
        x = st[s].pop("x")
        for j in range(d_ff // fc):
            gate = _dot(x, win_ref[:, j * fc:(j + 1) * fc])
            up = _dot(x, win_ref[:, d_ff + j * fc:d_ff + (j + 1) * fc])
            act_ref[rows[s], j * fc:(j + 1) * fc] = (_silu(gate) * up).astype(BF16)

    def down_mm(s):
        st[s]["y"] = _dot(act_ref[rows[s], :], wout_ref[...])

    def post_norm(s):
        st[s]["h"] = st[s]["h"] + 0.5 * _rmsnorm(st[s].pop("y"), gain(gb))

    def ple_mm(s):
        if with_ple:
            st[s]["z"] = _dot(st[s]["h"].astype(BF16), wg_ref[...])
            st[s]["proj"] = _dot(p_ref[rows[s], :].astype(BF16), wp_ref[...])

    def finish(s):
        h = st[s].pop("h")
        if with_ple:
            h = h + _rmsnorm(_sigmoid(st[s].pop("z")) * st[s].pop("proj"), gain(6))
        out_ref[rows[s], :] = h

    a, b = 0, 1
    for stage, s in [(mix_mm, a), (mix_mm, b), (pre_norm, a), (hidden, a), (pre_norm, b), (down_mm, a),
                     (hidden, b), (post_norm, a), (down_mm, b), (ple_mm, a), (post_norm, b), (finish, a),
                     (ple_mm, b), (finish, b)]:
        stage(s)


def _block(h, gains, layer, w_in, w_out, *, ga, gb, mix=None, ple=None, tm, interpret):
    t, d = h.shape
    d_ff = w_out.shape[1]
    os_, w_mix, mix_layer = mix if mix is not None else ([], None, None)
    args = [h, gains, *os_]
    specs = [_rows(tm, d), _layer(gains, layer)] + [_rows(tm, o.shape[1]) for o in os_]
    if os_:
        args.append(w_mix)
        specs.append(_layer(w_mix, mix_layer))
    args += [w_in, w_out]
    specs += [_layer(w_in, layer), _layer(w_out, layer)]
    if ple is not None:
        p_all, w_gate, w_proj = ple
        args += [p_all, w_gate, w_proj]
        specs += [pl.BlockSpec((None, tm, p_all.shape[2]), lambda i: (layer, i, 0)),
                  _layer(w_gate, layer), _layer(w_proj, layer)]
    return pl.pallas_call(
        functools.partial(_block_body, n_mix=len(os_), with_ple=ple is not None, ga=ga, gb=gb, fc=256, n_sub=2),
        grid=(t // tm,),
        in_specs=specs,
        out_specs=_rows(tm, d),
        out_shape=jax.ShapeDtypeStruct((t, d), F32),
        scratch_shapes=[pltpu.VMEM((tm, d_ff), BF16)],
        compiler_params=_params("parallel"),
        interpret=interpret, name="block",
    )(*args)


def _rope64(x, cos, sin_signed):
    n = x.shape[1]
    lane = lax.broadcasted_iota(jnp.int32, x.shape, 1)
    half = B_HEAD_DIM // 2
    swapped = jnp.where((lane % B_HEAD_DIM) < half, pltpu.roll(x, n - half, 1), pltpu.roll(x, half, 1))
    reps = n // cos.shape[1]
    cos_t = jnp.concatenate([cos] * reps, axis=1) if reps > 1 else cos
    sin_t = jnp.concatenate([sin_signed] * reps, axis=1) if reps > 1 else sin_signed
    return x * cos_t + swapped * sin_t


def _inproj_even_body(h_ref, g_ref, w_ref, lb_ref, cos_ref, sin_ref, y_ref, logf_ref):
    x = _rmsnorm(h_ref[...], g_ref[2:3, :]).astype(BF16)
    w = A_KW

    def proj(lo, hi):
        return _dot(x, w_ref[:, lo:hi])

    def put(lo, val):
        y_ref[:, lo:lo + val.shape[1]] = val.astype(y_ref.dtype)

    cos, sin = cos_ref[...], sin_ref[...]
    put(4 * w, _rope64(proj(4 * w, 4 * w + B_QW), cos, sin) * (B_HEAD_DIM ** -0.5 * LOG2E))
    kv = proj(4 * w + B_QW, 4 * w + B_QW + 2 * B_KW)
    put(4 * w + B_QW, _rope64(kv[:, :B_KW], cos, sin))
    put(4 * w + B_QW + B_KW, kv[:, B_KW:])
    lb = lb_ref[...]
    f = lb + (1.0 - lb) * _sigmoid(proj(w, 2 * w))
    logf_ref[...] = jnp.log(f)
    put(w, 1.0 - f)
    put(0, _silu(proj(0, w)))
    put(3 * w, _silu(proj(3 * w, 4 * w)))
    put(2 * w, proj(2 * w, 3 * w))


def _inproj_even(h, gains, layer, w, w_layer, lb, cos, sin, *, tm, interpret):
    t, d = h.shape
    n = w.shape[2]
    return pl.pallas_call(
        _inproj_even_body,
        grid=(t // tm,),
        in_specs=[_rows(tm, d), _layer(gains, layer), _layer(w, w_layer), _resident(lb.shape),
                  _rows(tm, LANES), _rows(tm, LANES)],
        out_specs=[_rows(tm, n), _rows(tm, A_KW)],
        out_shape=[jax.ShapeDtypeStruct((t, n), BF16), jax.ShapeDtypeStruct((t, A_KW), F32)],
        compiler_params=_params("parallel"),
        interpret=interpret, name="inproj_even",
    )(h, gains, w, lb, cos, sin)


def _inproj_odd_body(h_ref, g_ref, w_ref, cos_ref, sin_ref, y_ref):
    x = _rmsnorm(h_ref[...], g_ref[2:3, :]).astype(BF16)
    cos, sin = cos_ref[...], sin_ref[...]
    half = C_DK // 2

    def proj(lo, hi):
        return _dot(x, w_ref[:, lo:hi])

    def rope(v):
        x1, x2 = v[:, :half], v[:, half:]
        return jnp.concatenate([x1 * cos - x2 * sin, x2 * cos + x1 * sin], axis=-1)

    for hh in range(C_HEADS):
        lo = hh * C_DK
        y_ref[:, lo:lo + C_DK] = rope(proj(lo, lo + C_DK)).astype(y_ref.dtype)
        lo = C_KW + hh * C_DK
        y_ref[:, lo:lo + C_DK] = (rope(proj(lo, lo + C_DK)) * (C_DK ** -0.5)).astype(y_ref.dtype)
    step = 1024
    for lo in range(2 * C_KW + C_VW, 2 * C_KW + 2 * C_VW, step):
        y_ref[:, lo:lo + step] = _silu(proj(lo, lo + step)).astype(y_ref.dtype)
    for lo in range(2 * C_KW, 2 * C_KW + C_VW, step):
        y_ref[:, lo:lo + step] = proj(lo, lo + step).astype(y_ref.dtype)


def _inproj_odd(h, gains, layer, w, w_layer, cos, sin, *, tm, interpret):
    t, d = h.shape
    n = w.shape[2]
    return pl.pallas_call(
        _inproj_odd_body,
        grid=(t // tm,),
        in_specs=[_rows(tm, d), _layer(gains, layer), _layer(w, w_layer), _rows(tm, LANES), _rows(tm, LANES)],
        out_specs=_rows(tm, n),
        out_shape=jax.ShapeDtypeStruct((t, n), BF16),
        compiler_params=_params("parallel"),
        interpret=interpret, name="inproj_odd",
    )(h, gains, w, cos, sin)


def _hgrn_body(q_ref, k_ref, v_ref, g_ref, lf_ref, og_ref, o_ref, st_ref, qs_ref, kd_ref, acc_ref, dec_ref):
    ts = q_ref.shape[0]
    c, grp = A_CHUNK, 2 * LANES
    per_grp = grp // c

    @pl.when(pl.program_id(1) == 0)
    def _():
        st_ref[...] = jnp.zeros_like(st_ref)

    row = lax.broadcasted_iota(jnp.int32, (grp, grp), 0)
    col = lax.broadcasted_iota(jnp.int32, (grp, grp), 1)
    causal = ((row // c) == (col // c)) & (row >= col)
    ones_tril = causal.astype(BF16)

    groups = [slice(gi * grp, (gi + 1) * grp) for gi in range(ts // grp)]
    heads = [slice(hh * A_DK, (hh + 1) * A_DK) for hh in range(A_HEADS)]
    chunks = [slice(j * c, (j + 1) * c) for j in range(ts // c)]

    bs = []
    for rows in groups:
        logf = lf_ref[rows, :]
        logf_hi = logf.astype(BF16)
        logf_lo = (logf - logf_hi.astype(F32)).astype(BF16)
        bs.append(_dot(ones_tril, logf_hi) + _dot(ones_tril, logf_lo))
    q_s, k_s = [], []
    for gi, (rows, b) in enumerate(zip(groups, bs)):
        q_s.append((q_ref[rows, :].astype(F32) * jnp.exp(b)).astype(BF16))
        k_sf = k_ref[rows, :].astype(F32) * jnp.exp(-b)
        k_s.append(k_sf.astype(BF16))
        decs = [jnp.exp(b[j * c + c - 1:j * c + c, :]) for j in range(per_grp)]
        dec_rows = jnp.concatenate([jnp.broadcast_to(dj, (c, dj.shape[1])) for dj in decs], axis=0)
        kd_ref[rows, :] = (k_sf * dec_rows).astype(BF16)
        qs_ref[rows, :] = q_s[-1]
        for j, dj in enumerate(decs):
            dec_ref[gi * per_grp + j:gi * per_grp + j + 1, :] = dj

    upd = [[_dot_tn(v_ref[rws, sl], kd_ref[rws, sl]) for rws in chunks] for sl in heads]
    att = [[jnp.where(causal, _dot_nt(q_s[gi][:, sl], k_s[gi][:, sl]), 0.0).astype(BF16) for sl in heads]
           for gi in range(len(groups))]
    for gi, rows in enumerate(groups):
        for hh, sl in enumerate(heads):
            acc_ref[rows, sl] = _dot(att[gi][hh], v_ref[rows, sl])

    seen = []
    for hh, sl in enumerate(heads):
        st = st_ref[hh]
        seen.append([])
        for j in range(len(chunks)):
            seen[hh].append(st.astype(BF16))
            st = st * dec_ref[j:j + 1, sl] + upd[hh][j]
        st_ref[hh] = st

    inter = [[_dot_nt(qs_ref[rws, sl], seen[hh][j]) for j, rws in enumerate(chunks)] for hh, sl in enumerate(heads)]
    og = og_ref[...]
    for hh, sl in enumerate(heads):
        o = acc_ref[:, sl] + jnp.concatenate(inter[hh], axis=0)
        o_ref[:, sl] = (_rmsnorm(o, og) * g_ref[:, sl].astype(F32)).astype(o_ref.dtype)


def _hgrn(y, logf, og, *, batch, seq, ts, interpret):
    ns = seq // ts
    w = A_KW

    def col(cb):
        return pl.BlockSpec((ts, w), lambda b, s: (b * ns + s, cb))

    return pl.pallas_call(
        _hgrn_body,
        grid=(batch, ns),
        in_specs=[col(0), col(1), col(2), col(3), col(0), _resident((1, A_DV))],
        out_specs=col(0),
        out_shape=jax.ShapeDtypeStruct((batch * seq, A_VW), BF16),
        scratch_shapes=[pltpu.VMEM((A_HEADS, A_DV, A_DK), F32), pltpu.VMEM((ts, w), BF16),
                        pltpu.VMEM((ts, w), BF16), pltpu.VMEM((ts, A_VW), F32),
                        pltpu.VMEM((ts // A_CHUNK, w), F32)],
        compiler_params=_params("parallel", "arbitrary"),
        interpret=interpret, name="hgrn2",
    )(y, y, y, y, logf, og)


def _swa_body(sink_ref, q_ref, kc_ref, kp_ref, vc_ref, vp_ref, o_ref, kpad_ref, vext_ref):
    blk = B_BLOCK
    tq = q_ref.shape[0]
    rep = B_Q_HEADS // B_KV_HEADS
    tiles = [(g, pos) for g in range(B_KV_HEADS) for pos in range(2)]

    kk = jnp.concatenate([kp_ref[...], kc_ref[...]], axis=0).astype(F32)
    vv = jnp.concatenate([vp_ref[...], vc_ref[...]], axis=0).astype(F32)
    low = lax.broadcasted_iota(jnp.int32, kk.shape, 1) < B_HEAD_DIM
    kk_sw = pltpu.roll(kk, B_HEAD_DIM, 1)
    vv_sw = pltpu.roll(vv, B_HEAD_DIM, 1)
    for g, pos in tiles:
        keep = low if pos == 0 else ~low
        kpad_ref[2 * g + pos] = jnp.where(keep, kk if g == pos else kk_sw, 0.0).astype(BF16)
        vext_ref[2 * g + pos, :, :B_KW] = jnp.where(keep, vv if g == pos else vv_sw, 0.0).astype(BF16)
        vext_ref[2 * g + pos, :, B_KW:] = jnp.ones((tq + blk, LANES), BF16)

    qi = lax.broadcasted_iota(jnp.int32, (2 * blk, 2 * blk), 0) % blk
    kj = lax.broadcasted_iota(jnp.int32, (2 * blk, 2 * blk), 1)
    diff = qi + blk - kj
    band = (diff >= 0) & (diff < B_WINDOW)
    bias = jnp.where(band, 0.0, -jnp.inf)
    bias_first = jnp.where(band & ((kj + (pl.program_id(1) * tq - blk)) >= 0), 0.0, -jnp.inf)
    upper = lax.broadcasted_iota(jnp.int32, (2 * blk, 1), 0) < blk
    sinks = [jnp.where(upper, sink_ref[rep * g + pos], sink_ref[rep * g + 2 + pos]) * LOG2E for g, pos in tiles]

    for n in range(tq // blk):
        rows = slice(n * blk, (n + 1) * blk)
        keys = slice(n * blk, (n + 2) * blk)
        bn = bias_first if n == 0 else bias
        qs = [jnp.concatenate([q_ref[rows, j * LANES:(j + 1) * LANES] for j in (2 * g, 2 * g + 1)], axis=0)
              for g in range(B_KV_HEADS)]
        s = [_dot_nt(qs[g], kpad_ref[2 * g + pos, keys, :]) + bn for g, pos in tiles]
        m = [jnp.max(si, axis=-1, keepdims=True) for si in s]
        pr = [jnp.exp2(si - mi).astype(BF16) for si, mi in zip(s, m)]
        pv = [_dot(pi, vext_ref[2 * g + pos, keys, :]) for pi, (g, pos) in zip(pr, tiles)]
        res = [pvi[:, :LANES] / (pvi[:, LANES:] + jnp.exp2(sk - mi)) for pvi, sk, mi in zip(pv, sinks, m)]
        for j in range(B_QW // LANES):
            g, i = j // 2, j % 2
            oj = res[2 * g][i * blk:(i + 1) * blk] + res[2 * g + 1][i * blk:(i + 1) * blk]
            o_ref[rows, j * LANES:(j + 1) * LANES] = oj.astype(o_ref.dtype)


def _swa(y, sinks, *, batch, seq, tq, q_cb, k_cb, v_cb, interpret):
    blk = B_BLOCK
    nt = seq // tq
    per = tq // blk

    def cur(width, cb):
        return pl.BlockSpec((tq, width), lambda b, n: (b * nt + n, cb))

    def prev(width, cb):
        return pl.BlockSpec((blk, width), lambda b, n: (jnp.maximum((b * nt + n) * per - 1, 0), cb))

    return pl.pallas_call(
        _swa_body,
        grid=(batch, nt),
        in_specs=[pl.BlockSpec(memory_space=pltpu.SMEM),
                  cur(B_QW, q_cb), cur(B_KW, k_cb), prev(B_KW, k_cb), cur(B_KW, v_cb), prev(B_KW, v_cb)],
        out_specs=cur(B_QW, 0),
        out_shape=jax.ShapeDtypeStruct((batch * seq, B_QW), BF16),
        scratch_shapes=[pltpu.VMEM((4, tq + blk, B_KW), BF16), pltpu.VMEM((4, tq + blk, 2 * LANES), BF16)],
        compiler_params=_params("parallel", "arbitrary"),
        interpret=interpret, name="swa",
    )(sinks, y, y, y, y, y)


def _retention_body(q_ref, k_ref, v_ref, g_ref, og_ref, o_ref, st_ref, decay_ref, rq_ref, rk_ref, *, log_gammas):
    c = C_CHUNK

    @pl.when(pl.program_id(1) == 0)
    def _():
        st_ref[...] = jnp.zeros_like(st_ref)
        row = lax.broadcasted_iota(jnp.int32, (c, c), 0)
        col = lax.broadcasted_iota(jnp.int32, (c, c), 1)
        diff = (row - col).astype(F32)
        rowf = row[:, :LANES].astype(F32)
        for hh, lg in enumerate(log_gammas):
            decay_ref[hh] = jnp.where(diff >= 0, jnp.exp(jnp.maximum(diff, 0.0) * lg), 0.0)
            rq_ref[hh] = jnp.exp((rowf + 1.0) * lg)
            rk_ref[hh] = jnp.exp((c - 1.0 - rowf) * lg)

    nh = len(log_gammas)
    ksl = [slice(hh * C_DK, (hh + 1) * C_DK) for hh in range(nh)]
    vsl = [slice(hh * C_DV, (hh + 1) * C_DV) for hh in range(nh)]
    chunks = [slice(j * c, (j + 1) * c) for j in range(q_ref.shape[0] // c)]
    reps = C_DK // LANES
    att = [[(_dot_nt(q_ref[rows, ksl[hh]], k_ref[rows, ksl[hh]]) * decay_ref[hh]).astype(BF16) for hh in range(nh)]
           for rows in chunks]
    q_in = [[(q_ref[rows, ksl[hh]].astype(F32) * jnp.concatenate([rq_ref[hh]] * reps, axis=1)).astype(BF16)
             for hh in range(nh)] for rows in chunks]
    k_in = [[(k_ref[rows, ksl[hh]].astype(F32) * jnp.concatenate([rk_ref[hh]] * reps, axis=1)).astype(BF16)
             for hh in range(nh)] for rows in chunks]
    upd = [[_dot_tn(k_in[j][hh], v_ref[rows, vsl[hh]]) for hh in range(nh)] for j, rows in enumerate(chunks)]
    st = [st_ref[hh] for hh in range(nh)]
    o = []
    for j, rows in enumerate(chunks):
        o.append([_dot(att[j][hh], v_ref[rows, vsl[hh]]) + _dot(q_in[j][hh], st[hh].astype(BF16))
                  for hh in range(nh)])
        st = [st[hh] * float(np.exp(c * lg)) + upd[j][hh] for hh, lg in enumerate(log_gammas)]
    for hh in range(nh):
        st_ref[hh] = st[hh]
    og = og_ref[...]
    for j, rows in enumerate(chunks):
        for hh in range(nh):
            gate = g_ref[rows, vsl[hh]].astype(F32)
            o_ref[rows, vsl[hh]] = (_rmsnorm(o[j][hh], og) * gate).astype(o_ref.dtype)


def _retention(y, og, *, batch, seq, ts, interpret):
    c = C_CHUNK
    ns = seq // ts
    log_gammas = tuple(float(v) for v in np.log1p(-np.exp2(-5.0 - np.arange(C_HEADS, dtype=np.float64))))

    def blk(width, cb):
        return pl.BlockSpec((ts, width), lambda b, s: (b * ns + s, cb))

    return pl.pallas_call(
        functools.partial(_retention_body, log_gammas=log_gammas),
        grid=(batch, ns),
        in_specs=[blk(C_KW, 0), blk(C_KW, 1), blk(C_VW, 1), blk(C_VW, 2), _resident((1, C_DV))],
        out_specs=blk(C_VW, 0),
        out_shape=jax.ShapeDtypeStruct((batch * seq, C_VW), BF16),
        scratch_shapes=[pltpu.VMEM((C_HEADS, C_DK, C_DV), F32), pltpu.VMEM((C_HEADS, c, c), F32),
                        pltpu.VMEM((C_HEADS, c, LANES), F32), pltpu.VMEM((C_HEADS, c, LANES), F32)],
        compiler_params=_params("parallel", "arbitrary"),
        interpret=interpret, name="retention",
    )(y, y, y, y, og)


def _forward(x, p, positions, norm_g, ffn1_w_in, ffn1_w_out, ffn2_w_in, ffn2_w_out, ple_w_proj, ple_w_gate,
             even_w_in, even_w_out, hgrn_lb, hgrn_onorm_g, attn_sinks, odd_w_in, odd_w_out, ret_onorm_g,
             *, interpret=False):
    batch, seq, d = x.shape
    t = batch * seq
    depth = norm_g.shape[0]
    tm = min(512, t)
    kw = dict(interpret=interpret)

    pos = positions.astype(F32).reshape(t, 1)
    inv_b = ROPE_THETA ** (-jnp.arange(0, B_HEAD_DIM, 2, dtype=F32) / B_HEAD_DIM)
    inv_c = RET_THETA ** (-jnp.linspace(0.0, 1.0, C_DK // 2, dtype=F32))
    ang_b = pos * inv_b[None, :]
    ang_c = pos * inv_c[None, :]
    cos_b = jnp.tile(jnp.cos(ang_b), (1, 4))
    sin_b = jnp.tile(jnp.concatenate([-jnp.sin(ang_b), jnp.sin(ang_b)], axis=1), (1, 2))
    cos_c, sin_c = jnp.cos(ang_c), jnp.sin(ang_c)
    lb_all = jnp.cumsum(jax.nn.softmax(hgrn_lb.astype(F32), axis=0), axis=0)
    p_all = p.reshape(depth, t, p.shape[-1])

    w1_in, w1_out, w2_in, w2_out = (w.astype(BF16) for w in (ffn1_w_in, ffn1_w_out, ffn2_w_in, ffn2_w_out))
    wp_gate, wp_proj = ple_w_gate.astype(BF16), ple_w_proj.astype(BF16)
    we_in, we_out, wo_in, wo_out = (w.astype(BF16) for w in (even_w_in, even_w_out, odd_w_in, odd_w_out))

    h = x.reshape(t, d)
    for i in range(depth):
        j = i // 2
        h = _block(h, norm_g, i, w1_in, w1_out, ga=0, gb=1, tm=tm, **kw)
        if i % 2 == 0:
            y, logf = _inproj_even(h, norm_g, i, we_in, j, lb_all[j][None, :], cos_b, sin_b, tm=tm, **kw)
            oa = _hgrn(y, logf, hgrn_onorm_g[j][None, :], batch=batch, seq=seq, ts=min(512, seq), **kw)
            q_cb = (2 * A_KW + 2 * A_VW) // B_QW
            k_cb = (2 * A_KW + 2 * A_VW + B_QW) // B_KW
            ob = _swa(y, attn_sinks[j].astype(F32), batch=batch, seq=seq, tq=min(512, seq),
                      q_cb=q_cb, k_cb=k_cb, v_cb=k_cb + 1, **kw)
            mix = ([oa, ob], we_out, j)
        else:
            y = _inproj_odd(h, norm_g, i, wo_in, j, cos_c, sin_c, tm=tm, **kw)
            o = _retention(y, ret_onorm_g[j][None, :], batch=batch, seq=seq, ts=min(2 * C_CHUNK, seq), **kw)
            mix = ([o], wo_out, j)
        h = _block(h, norm_g, i, w2_in, w2_out, ga=4, gb=5, mix=mix, ple=(p_all, wp_gate, wp_proj), tm=tm, **kw)
    return h.reshape(batch, seq, d)


def kernel(x, p, positions, norm_g, ffn1_w_in, ffn1_w_out, ffn2_w_in, ffn2_w_out, ple_w_proj, ple_w_gate,
           even_w_in, even_w_out, hgrn_lb, hgrn_onorm_g, attn_sinks, odd_w_in, odd_w_out, ret_onorm_g):
    return _forward(x, p, positions, norm_g, ffn1_w_in, ffn1_w_out, ffn2_w_in, ffn2_w_out, ple_w_proj, ple_w_gate,
                    even_w_in, even_w_out, hgrn_lb, hgrn_onorm_g, attn_sinks, odd_w_in, odd_w_out, ret_onorm_g)
```

```python
import functools

import numpy as np
import jax
import jax.numpy as jnp
from jax import lax
from jax.experimental import pallas as pl
from jax.experimental.pallas import tpu as pltpu

F32 = jnp.float32
BF16 = jnp.bfloat16

EPS = 1e-6
ROPE_THETA = 10000.0
RET_THETA = 10000.0
LOG2E = 1.4426950408889634

A_HEADS, A_DK, A_DV, A_CHUNK = 4, 128, 128, 64
B_Q_HEADS, B_KV_HEADS, B_HEAD_DIM, B_WINDOW, B_BLOCK = 8, 2, 64, 128, 128
C_HEADS, C_DK, C_DV = 4, 256, 512
C_CHUNK = 256

A_KW = A_HEADS * A_DK
A_VW = A_HEADS * A_DV
B_QW = B_Q_HEADS * B_HEAD_DIM
B_KW = B_KV_HEADS * B_HEAD_DIM
C_KW = C_HEADS * C_DK
C_VW = C_HEADS * C_DV

LANES = 128
VMEM_LIMIT_BYTES = 56 * 1024 * 1024

NT_DIMS = (((1,), (1,)), ((), ()))
TN_DIMS = (((0,), (0,)), ((), ()))


def _rmsnorm(x, g):
    return x * lax.rsqrt(jnp.mean(x * x, axis=-1, keepdims=True) + EPS) * g


def _sigmoid(x):
    return 0.5 * jnp.tanh(0.5 * x) + 0.5


def _silu(x):
    hx = 0.5 * x
    return hx + hx * jnp.tanh(hx)


def _dot(a, b):
    return jnp.dot(a, b, preferred_element_type=F32)


def _dot_nt(a, b):
    return lax.dot_general(a, b, NT_DIMS, preferred_element_type=F32)


def _dot_tn(a, b):
    return lax.dot_general(a, b, TN_DIMS, preferred_element_type=F32)


def _params(*sem):
    return pltpu.CompilerParams(dimension_semantics=sem, vmem_limit_bytes=VMEM_LIMIT_BYTES)


def _resident(shape):
    return pl.BlockSpec(shape, lambda *_: (0,) * len(shape), pipeline_mode=pl.Buffered(1))


def _layer(stacked, layer):
    shape = stacked.shape[1:]
    return pl.BlockSpec((None,) + shape, lambda *_: (layer,) + (0,) * len(shape), pipeline_mode=pl.Buffered(1))


def _rows(tm, width):
    return pl.BlockSpec((tm, width), lambda i: (i, 0))


def _block_body(*refs, n_mix, with_ple, ga, gb, fc, n_sub):
    it = iter(refs)
    h_ref, g_ref = next(it), next(it)
    o_refs = [next(it) for _ in range(n_mix)]
    wo_ref = next(it) if n_mix else None
    win_ref, wout_ref = next(it), next(it)
    if with_ple:
        p_ref, wg_ref, wp_ref = next(it), next(it), next(it)
    out_ref, act_ref = next(it), next(it)
    d_ff = wout_ref.shape[0]
    sub = h_ref.shape[0] // n_sub

    def gain(k):
        return g_ref[k:k + 1, :]

    rows = [slice(s * sub, (s + 1) * sub) for s in range(n_sub)]
    st = [dict() for _ in range(n_sub)]

    def mix_mm(s):
        if n_mix:
            mix, lo = None, 0
            for o_ref in o_refs:
                part = _dot(o_ref[rows[s], :], wo_ref[lo:lo + o_ref.shape[1], :])
                mix = part if mix is None else mix + part
                lo += o_ref.shape[1]
            st[s]["mix"] = mix

    def pre_norm(s):
        h = h_ref[rows[s], :]
        if n_mix:
            h = h + _rmsnorm(st[s].pop("mix"), gain(3))
        st[s]["h"] = h
        st[s]["x"] = _rmsnorm(h, gain(ga)).astype(BF16)

    def hidden(s):
        x = st[s].pop("x")
        for j in range(d_ff // fc):
            gate = _dot(x, win_ref[:, j * fc:(j + 1) * fc])
            up = _dot(x, win_ref[:, d_ff + j * fc:d_ff + (j + 1) * fc])
            act_ref[rows[s], j * fc:(j + 1) * fc] = (_silu(gate) * up).astype(BF16)

    def down_mm(s):
        st[s]["y"] = _dot(act_ref[rows[s], :], wout_ref[...])

    def post_norm(s):
        st[s]["h"] = st[s]["h"] + 0.5 * _rmsnorm(st[s].pop("y"), gain(gb))

    def ple_mm(s):
        if with_ple:
            st[s]["z"] = _dot(st[s]["h"].astype(BF16), wg_ref[...])
            st[s]["proj"] = _dot(p_ref[rows[s], :].astype(BF16), wp_ref[...])

    def finish(s):
        h = st[s].pop("h")
        if with_ple:
            h = h + _rmsnorm(_sigmoid(st[s].pop("z")) * st[s].pop("proj"), gain(6))
        out_ref[rows[s], :] = h

    a, b = 0, 1
    for stage, s in [(mix_mm, a), (mix_mm, b), (pre_norm, a), (hidden, a), (pre_norm, b), (down_mm, a),
                     (hidden, b), (post_norm, a), (down_mm, b), (ple_mm, a), (post_norm, b), (finish, a),
                     (ple_mm, b), (finish, b)]:
        stage(s)


def _block(h, gains, layer, w_in, w_out, *, ga, gb, mix=None, ple=None, tm, interpret):
    t, d = h.shape
    d_ff = w_out.shape[1]
    os_, w_mix, mix_layer = mix if mix is not None else ([], None, None)
    args = [h, gains, *os_]
    specs = [_rows(tm, d), _layer(gains, layer)] + [_rows(tm, o.shape[1]) for o in os_]
    if os_:
        args.append(w_mix)
        specs.append(_layer(w_mix, mix_layer))
    args += [w_in, w_out]
    specs += [_layer(w_in, layer), _layer(w_out, layer)]
    if ple is not None:
        p_all, w_gate, w_proj = ple
        args += [p_all, w_gate, w_proj]
        specs += [pl.BlockSpec((None, tm, p_all.shape[2]), lambda i: (layer, i, 0)),
                  _layer(w_gate, layer), _layer(w_proj, layer)]
    return pl.pallas_call(
        functools.partial(_block_body, n_mix=len(os_), with_ple=ple is not None, ga=ga, gb=gb, fc=256, n_sub=2),
        grid=(t // tm,),
        in_specs=specs,
        out_specs=_rows(tm, d),
        out_shape=jax.ShapeDtypeStruct((t, d), F32),
        scratch_shapes=[pltpu.VMEM((tm, d_ff), BF16)],
        compiler_params=_params("parallel"),
        interpret=interpret, name="block",
    )(*args)


def _rope64(x, cos, sin_signed):
    n = x.shape[1]
    lane = lax.broadcasted_iota(jnp.int32, x.shape, 1)
    half = B_HEAD_DIM // 2
    swapped = jnp.where((lane % B_HEAD_DIM) < half, pltpu.roll(x, n - half, 1), pltpu.roll(x, half, 1))
    reps = n // cos.shape[1]
    cos_t = jnp.concatenate([cos] * reps, axis=1) if reps > 1 else cos
    sin_t = jnp.concatenate([sin_signed] * reps, axis=1) if reps > 1 else sin_signed
    return x * cos_t + swapped * sin_t


def _inproj_even_body(h_ref, g_ref, w_ref, lb_ref, cos_ref, sin_ref, y_ref, logf_ref):
    x = _rmsnorm(h_ref[...], g_ref[2:3, :]).astype(BF16)
    w = A_KW

    def proj(lo, hi):
        return _dot(x, w_ref[:, lo:hi])

    def put(lo, val):
        y_ref[:, lo:lo + val.shape[1]] = val.astype(y_ref.dtype)

    cos, sin = cos_ref[...], sin_ref[...]
    put(4 * w, _rope64(proj(4 * w, 4 * w + B_QW), cos, sin) * (B_HEAD_DIM ** -0.5 * LOG2E))
    kv = proj(4 * w + B_QW, 4 * w + B_QW + 2 * B_KW)
    put(4 * w + B_QW, _rope64(kv[:, :B_KW], cos, sin))
    put(4 * w + B_QW + B_KW, kv[:, B_KW:])
    lb = lb_ref[...]
    f = lb + (1.0 - lb) * _sigmoid(proj(w, 2 * w))
    logf_ref[...] = jnp.log(f)
    put(w, 1.0 - f)
    put(0, _silu(proj(0, w)))
    put(3 * w, _silu(proj(3 * w, 4 * w)))
    put(2 * w, proj(2 * w, 3 * w))


def _inproj_even(h, gains, layer, w, w_layer, lb, cos, sin, *, tm, interpret):
    t, d = h.shape
    n = w.shape[2]
    return pl.pallas_call(
        _inproj_even_body,
        grid=(t // tm,),
        in_specs=[_rows(tm, d), _layer(gains, layer), _layer(w, w_layer), _resident(lb.shape),
                  _rows(tm, LANES), _rows(tm, LANES)],
        out_specs=[_rows(tm, n), _rows(tm, A_KW)],
        out_shape=[jax.ShapeDtypeStruct((t, n), BF16), jax.ShapeDtypeStruct((t, A_KW), F32)],
        compiler_params=_params("parallel"),
        interpret=interpret, name="inproj_even",
    )(h, gains, w, lb, cos, sin)


def _inproj_odd_body(h_ref, g_ref, w_ref, cos_ref, sin_ref, y_ref):
    x = _rmsnorm(h_ref[...], g_ref[2:3, :]).astype(BF16)
    cos, sin = cos_ref[...], sin_ref[...]
    half = C_DK // 2

    def proj(lo, hi):
        return _dot(x, w_ref[:, lo:hi])

    def rope(v):
        x1, x2 = v[:, :half], v[:, half:]
        return jnp.concatenate([x1 * cos - x2 * sin, x2 * cos + x1 * sin], axis=-1)

    for hh in range(C_HEADS):
        lo = hh * C_DK
        y_ref[:, lo:lo + C_DK] = rope(proj(lo, lo + C_DK)).astype(y_ref.dtype)
        lo = C_KW + hh * C_DK
        y_ref[:, lo:lo + C_DK] = (rope(proj(lo, lo + C_DK)) * (C_DK ** -0.5)).astype(y_ref.dtype)
    step = 1024
    for lo in range(2 * C_KW + C_VW, 2 * C_KW + 2 * C_VW, step):
        y_ref[:, lo:lo + step] = _silu(proj(lo, lo + step)).astype(y_ref.dtype)
    for lo in range(2 * C_KW, 2 * C_KW + C_VW, step):
        y_ref[:, lo:lo + step] = proj(lo, lo + step).astype(y_ref.dtype)


def _inproj_odd(h, gains, layer, w, w_layer, cos, sin, *, tm, interpret):
    t, d = h.shape
    n = w.shape[2]
    return pl.pallas_call(
        _inproj_odd_body,
        grid=(t // tm,),
        in_specs=[_rows(tm, d), _layer(gains, layer), _layer(w, w_layer), _rows(tm, LANES), _rows(tm, LANES)],
        out_specs=_rows(tm, n),
        out_shape=jax.ShapeDtypeStruct((t, n), BF16),
        compiler_params=_params("parallel"),
        interpret=interpret, name="inproj_odd",
    )(h, gains, w, cos, sin)


def _hgrn_body(q_ref, k_ref, v_ref, g_ref, lf_ref, og_ref, o_ref, st_ref, qs_ref, kd_ref, acc_ref, dec_ref):
    ts = q_ref.shape[0]
    c, grp = A_CHUNK, 2 * LANES
    per_grp = grp // c

    @pl.when(pl.program_id(1) == 0)
    def _():
        st_ref[...] = jnp.zeros_like(st_ref)

    row = lax.broadcasted_iota(jnp.int32, (grp, grp), 0)
    col = lax.broadcasted_iota(jnp.int32, (grp, grp), 1)
    causal = ((row // c) == (col // c)) & (row >= col)
    ones_tril = causal.astype(BF16)

    groups = [slice(gi * grp, (gi + 1) * grp) for gi in range(ts // grp)]
    heads = [slice(hh * A_DK, (hh + 1) * A_DK) for hh in range(A_HEADS)]
    chunks = [slice(j * c, (j + 1) * c) for j in range(ts // c)]

    bs = []
    for rows in groups:
        logf = lf_ref[rows, :]
        logf_hi = logf.astype(BF16)
        logf_lo = (logf - logf_hi.astype(F32)).astype(BF16)
        bs.append(_dot(ones_tril, logf_hi) + _dot(ones_tril, logf_lo))
    q_s, k_s = [], []
    for gi, (rows, b) in enumerate(zip(groups, bs)):
        q_s.append((q_ref[rows, :].astype(F32) * jnp.exp(b)).astype(BF16))
        k_sf = k_ref[rows, :].astype(F32) * jnp.exp(-b)
        k_s.append(k_sf.astype(BF16))
        decs = [jnp.exp(b[j * c + c - 1:j * c + c, :]) for j in range(per_grp)]
        dec_rows = jnp.concatenate([jnp.broadcast_to(dj, (c, dj.shape[1])) for dj in decs], axis=0)
        kd_ref[rows, :] = (k_sf * dec_rows).astype(BF16)
        qs_ref[rows, :] = q_s[-1]
        for j, dj in enumerate(decs):
            dec_ref[gi * per_grp + j:gi * per_grp + j + 1, :] = dj

    upd = [[_dot_tn(v_ref[rws, sl], kd_ref[rws, sl]) for rws in chunks] for sl in heads]
    att = [[jnp.where(causal, _dot_nt(q_s[gi][:, sl], k_s[gi][:, sl]), 0.0).astype(BF16) for sl in heads]
           for gi in range(len(groups))]
    for gi, rows in enumerate(groups):
        for hh, sl in enumerate(heads):
            acc_ref[rows, sl] = _dot(att[gi][hh], v_ref[rows, sl])

    seen = []
    for hh, sl in enumerate(heads):
        st = st_ref[hh]
        seen.append([])
        for j in range(len(chunks)):
            seen[hh].append(st.astype(BF16))
            st = st * dec_ref[j:j + 1, sl] + upd[hh][j]
        st_ref[hh] = st

    inter = [[_dot_nt(qs_ref[rws, sl], seen[hh][j]) for j, rws in enumerate(chunks)] for hh, sl in enumerate(heads)]
    og = og_ref[...]
    for hh, sl in enumerate(heads):
        o = acc_ref[:, sl] + jnp.concatenate(inter[hh], axis=0)
        o_ref[:, sl] = (_rmsnorm(o, og) * g_ref[:, sl].astype(F32)).astype(o_ref.dtype)


def _hgrn(y, logf, og, *, batch, seq, ts, interpret):
    ns = seq // ts
    w = A_KW

    def col(cb):
        return pl.BlockSpec((ts, w), lambda b, s: (b * ns + s, cb))

    return pl.pallas_call(
        _hgrn_body,
        grid=(batch, ns),
        in_specs=[col(0), col(1), col(2), col(3), col(0), _resident((1, A_DV))],
        out_specs=col(0),
        out_shape=jax.ShapeDtypeStruct((batch * seq, A_VW), BF16),
        scratch_shapes=[pltpu.VMEM((A_HEADS, A_DV, A_DK), F32), pltpu.VMEM((ts, w), BF16),
                        pltpu.VMEM((ts, w), BF16), pltpu.VMEM((ts, A_VW), F32),
                        pltpu.VMEM((ts // A_CHUNK, w), F32)],
        compiler_params=_params("parallel", "arbitrary"),
        interpret=interpret, name="hgrn2",
    )(y, y, y, y, logf, og)


def _swa_body(sink_ref, q_ref, kc_ref, kp_ref, vc_ref, vp_ref, o_ref, kpad_ref, vext_ref):
    blk = B_BLOCK
    tq = q_ref.shape[0]
    rep = B_Q_HEADS // B_KV_HEADS
    tiles = [(g, pos) for g in range(B_KV_HEADS) for pos in range(2)]

    kk = jnp.concatenate([kp_ref[...], kc_ref[...]], axis=0).astype(F32)
    vv = jnp.concatenate([vp_ref[...], vc_ref[...]], axis=0).astype(F32)
    low = lax.broadcasted_iota(jnp.int32, kk.shape, 1) < B_HEAD_DIM
    kk_sw = pltpu.roll(kk, B_HEAD_DIM, 1)
    vv_sw = pltpu.roll(vv, B_HEAD_DIM, 1)
    for g, pos in tiles:
        keep = low if pos == 0 else ~low
        kpad_ref[2 * g + pos] = jnp.where(keep, kk if g == pos else kk_sw, 0.0).astype(BF16)
        vext_ref[2 * g + pos, :, :B_KW] = jnp.where(keep, vv if g == pos else vv_sw, 0.0).astype(BF16)
        vext_ref[2 * g + pos, :, B_KW:] = jnp.ones((tq + blk, LANES), BF16)

    qi = lax.broadcasted_iota(jnp.int32, (2 * blk, 2 * blk), 0) % blk
    kj = lax.broadcasted_iota(jnp.int32, (2 * blk, 2 * blk), 1)
    diff = qi + blk - kj
    band = (diff >= 0) & (diff < B_WINDOW)
    bias = jnp.where(band, 0.0, -jnp.inf)
    bias_first = jnp.where(band & ((kj + (pl.program_id(1) * tq - blk)) >= 0), 0.0, -jnp.inf)
    upper = lax.broadcasted_iota(jnp.int32, (2 * blk, 1), 0) < blk
    sinks = [jnp.where(upper, sink_ref[rep * g + pos], sink_ref[rep * g + 2 + pos]) * LOG2E for g, pos in tiles]

    for n in range(tq // blk):
        rows = slice(n * blk, (n + 1) * blk)
        keys = slice(n * blk, (n + 2) * blk)
        bn = bias_first if n == 0 else bias
        qs = [jnp.concatenate([q_ref[rows, j * LANES:(j + 1) * LANES] for j in (2 * g, 2 * g + 1)], axis=0)
              for g in range(B_KV_HEADS)]
        s = [_dot_nt(qs[g], kpad_ref[2 * g + pos, keys, :]) + bn for g, pos in tiles]
        m = [jnp.max(si, axis=-1, keepdims=True) for si in s]
        pr = [jnp.exp2(si - mi).astype(BF16) for si, mi in zip(s, m)]
        pv = [_dot(pi, vext_ref[2 * g + pos, keys, :]) for pi, (g, pos) in zip(pr, tiles)]
        res = [pvi[:, :LANES] / (pvi[:, LANES:] + jnp.exp2(sk - mi)) for pvi, sk, mi in zip(pv, sinks, m)]
        for j in range(B_QW // LANES):
            g, i = j // 2, j % 2
            oj = res[2 * g][i * blk:(i + 1) * blk] + res[2 * g + 1][i * blk:(i + 1) * blk]
            o_ref[rows, j * LANES:(j + 1) * LANES] = oj.astype(o_ref.dtype)


def _swa(y, sinks, *, batch, seq, tq, q_cb, k_cb, v_cb, interpret):
    blk = B_BLOCK
    nt = seq // tq
    per = tq // blk

    def cur(width, cb):
        return pl.BlockSpec((tq, width), lambda b, n: (b * nt + n, cb))

    def prev(width, cb):
        return pl.BlockSpec((blk, width), lambda b, n: (jnp.maximum((b * nt + n) * per - 1, 0), cb))

    return pl.pallas_call(
        _swa_body,
        grid=(batch, nt),
        in_specs=[pl.BlockSpec(memory_space=pltpu.SMEM),
                  cur(B_QW, q_cb), cur(B_KW, k_cb), prev(B_KW, k_cb), cur(B_KW, v_cb), prev(B_KW, v_cb)],
        out_specs=cur(B_QW, 0),
        out_shape=jax.ShapeDtypeStruct((batch * seq, B_QW), BF16),
        scratch_shapes=[pltpu.VMEM((4, tq + blk, B_KW), BF16), pltpu.VMEM((4, tq + blk, 2 * LANES), BF16)],
        compiler_params=_params("parallel", "arbitrary"),
        interpret=interpret, name="swa",
    )(sinks, y, y, y, y, y)


def _retention_body(q_ref, k_ref, v_ref, g_ref, og_ref, o_ref, st_ref, decay_ref, rq_ref, rk_ref, *, log_gammas):
    c = C_CHUNK

    @pl.when(pl.program_id(1) == 0)
    def _():
        st_ref[...] = jnp.zeros_like(st_ref)
        row = lax.broadcasted_iota(jnp.int32, (c, c), 0)
        col = lax.broadcasted_iota(jnp.int32, (c, c), 1)
        diff = (row - col).astype(F32)
        rowf = row[:, :LANES].astype(F32)
        for hh, lg in enumerate(log_gammas):
            decay_ref[hh] = jnp.where(diff >= 0, jnp.exp(jnp.maximum(diff, 0.0) * lg), 0.0)
            rq_ref[hh] = jnp.exp((rowf + 1.0) * lg)
            rk_ref[hh] = jnp.exp((c - 1.0 - rowf) * lg)

    nh = len(log_gammas)
    ksl = [slice(hh * C_DK, (hh + 1) * C_DK) for hh in range(nh)]
    vsl = [slice(hh * C_DV, (hh + 1) * C_DV) for hh in range(nh)]
    chunks = [slice(j * c, (j + 1) * c) for j in range(q_ref.shape[0] // c)]
    reps = C_DK // LANES
    att = [[(_dot_nt(q_ref[rows, ksl[hh]], k_ref[rows, ksl[hh]]) * decay_ref[hh]).astype(BF16) for hh in range(nh)]
           for rows in chunks]
    q_in = [[(q_ref[rows, ksl[hh]].astype(F32) * jnp.concatenate([rq_ref[hh]] * reps, axis=1)).astype(BF16)
             for hh in range(nh)] for rows in chunks]
    k_in = [[(k_ref[rows, ksl[hh]].astype(F32) * jnp.concatenate([rk_ref[hh]] * reps, axis=1)).astype(BF16)
             for hh in range(nh)] for rows in chunks]
    upd = [[_dot_tn(k_in[j][hh], v_ref[rows, vsl[hh]]) for hh in range(nh)] for j, rows in enumerate(chunks)]
    st = [st_ref[hh] for hh in range(nh)]
    o = []
    for j, rows in enumerate(chunks):
        o.append([_dot(att[j][hh], v_ref[rows, vsl[hh]]) + _dot(q_in[j][hh], st[hh].astype(BF16))
                  for hh in range(nh)])
        st = [st[hh] * float(np.exp(c * lg)) + upd[j][hh] for hh, lg in enumerate(log_gammas)]
    for hh in range(nh):
        st_ref[hh] = st[hh]
    og = og_ref[...]
    for j, rows in enumerate(chunks):
        for hh in range(nh):
            gate = g_ref[rows, vsl[hh]].astype(F32)
            o_ref[rows, vsl[hh]] = (_rmsnorm(o[j][hh], og) * gate).astype(o_ref.dtype)


def _retention(y, og, *, batch, seq, ts, interpret):
    c = C_CHUNK
    ns = seq // ts
    log_gammas = tuple(float(v) for v in np.log1p(-np.exp2(-5.0 - np.arange(C_HEADS, dtype=np.float64))))

    def blk(width, cb):
        return pl.BlockSpec((ts, width), lambda b, s: (b * ns + s, cb))

    return pl.pallas_call(
        functools.partial(_retention_body, log_gammas=log_gammas),
        grid=(batch, ns),
        in_specs=[blk(C_KW, 0), blk(C_KW, 1), blk(C_VW, 1), blk(C_VW, 2), _resident((1, C_DV))],
        out_specs=blk(C_VW, 0),
        out_shape=jax.ShapeDtypeStruct((batch * seq, C_VW), BF16),
        scratch_shapes=[pltpu.VMEM((C_HEADS, C_DK, C_DV), F32), pltpu.VMEM((C_HEADS, c, c), F32),
                        pltpu.VMEM((C_HEADS, c, LANES), F32), pltpu.VMEM((C_HEADS, c, LANES), F32)],
        compiler_params=_params("parallel", "arbitrary"),
        interpret=interpret, name="retention",
    )(y, y, y, y, og)


def _forward(x, p, positions, norm_g, ffn1_w_in, ffn1_w_out, ffn2_w_in, ffn2_w_out, ple_w_proj, ple_w_gate,
             even_w_in, even_w_out, hgrn_lb, hgrn_onorm_g, attn_sinks, odd_w_in, odd_w_out, ret_onorm_g,
             *, interpret=False):
    batch, seq, d = x.shape
    t = batch * seq
    depth = norm_g.shape[0]
    tm = min(512, t)
    kw = dict(interpret=interpret)

    pos = positions.astype(F32).reshape(t, 1)
    inv_b = ROPE_THETA ** (-jnp.arange(0, B_HEAD_DIM, 2, dtype=F32) / B_HEAD_DIM)
    inv_c = RET_THETA ** (-jnp.linspace(0.0, 1.0, C_DK // 2, dtype=F32))
    ang_b = pos * inv_b[None, :]
    ang_c = pos * inv_c[None, :]
    cos_b = jnp.tile(jnp.cos(ang_b), (1, 4))
    sin_b = jnp.tile(jnp.concatenate([-jnp.sin(ang_b), jnp.sin(ang_b)], axis=1), (1, 2))
    cos_c, sin_c = jnp.cos(ang_c), jnp.sin(ang_c)
    lb_all = jnp.cumsum(jax.nn.softmax(hgrn_lb.astype(F32), axis=0), axis=0)
    p_all = p.reshape(depth, t, p.shape[-1])

    w1_in, w1_out, w2_in, w2_out = (w.astype(BF16) for w in (ffn1_w_in, ffn1_w_out, ffn2_w_in, ffn2_w_out))
    wp_gate, wp_proj = ple_w_gate.astype(BF16), ple_w_proj.astype(BF16)
    we_in, we_out, wo_in, wo_out = (w.astype(BF16) for w in (even_w_in, even_w_out, odd_w_in, odd_w_out))

    h = x.reshape(t, d)
    for i in range(depth):
        j = i // 2
        h = _block(h, norm_g, i, w1_in, w1_out, ga=0, gb=1, tm=min(1024, t), **kw)
        if i % 2 == 0:
            y, logf = _inproj_even(h, norm_g, i, we_in, j, lb_all[j][None, :], cos_b, sin_b, tm=min(1024, t), **kw)
            oa = _hgrn(y, logf, hgrn_onorm_g[j][None, :], batch=batch, seq=seq, ts=min(1024, seq), **kw)
            q_cb = (2 * A_KW + 2 * A_VW) // B_QW
            k_cb = (2 * A_KW + 2 * A_VW + B_QW) // B_KW
            ob = _swa(y, attn_sinks[j].astype(F32), batch=batch, seq=seq, tq=min(1024, seq),
                      q_cb=q_cb, k_cb=k_cb, v_cb=k_cb + 1, **kw)
            mix = ([oa, ob], we_out, j)
        else:
            y = _inproj_odd(h, norm_g, i, wo_in, j, cos_c, sin_c, tm=tm, **kw)
            o = _retention(y, ret_onorm_g[j][None, :], batch=batch, seq=seq, ts=min(4 * C_CHUNK, seq), **kw)
            mix = ([o], wo_out, j)
        h = _block(h, norm_g, i, w2_in, w2_out, ga=4, gb=5, mix=mix, ple=(p_all, wp_gate, wp_proj), tm=tm, **kw)
    return h.reshape(batch, seq, d)


def kernel(x, p, positions, norm_g, ffn1_w_in, ffn1_w_out, ffn2_w_in, ffn2_w_out, ple_w_proj, ple_w_gate,
           even_w_in, even_w_out, hgrn_lb, hgrn_onorm_g, attn_sinks, odd_w_in, odd_w_out, ret_onorm_g):
    return _forward(x, p, positions, norm_g, ffn1_w_in, ffn1_w_out, ffn2_w_in, ffn2_w_out, ple_w_proj, ple_w_gate,
                    even_w_in, even_w_out, hgrn_lb, hgrn_onorm_g, attn_sinks, odd_w_in, odd_w_out, ret_onorm_g)
```

```python
import functools

import numpy as np
import jax
import jax.numpy as jnp
from jax import lax
from jax.experimental import pallas as pl
from jax.experimental.pallas import tpu as pltpu

F32 = jnp.float32
BF16 = jnp.bfloat16

EPS = 1e-6
ROPE_THETA = 10000.0
RET_THETA = 10000.0
LOG2E = 1.4426950408889634

A_HEADS, A_DK, A_DV, A_CHUNK = 4, 128, 128, 64
B_Q_HEADS, B_KV_HEADS, B_HEAD_DIM, B_WINDOW, B_BLOCK = 8, 2, 64, 128, 128
C_HEADS, C_DK, C_DV = 4, 256, 512
C_CHUNK = 256

A_KW = A_HEADS * A_DK
A_VW = A_HEADS * A_DV
B_QW = B_Q_HEADS * B_HEAD_DIM
B_KW = B_KV_HEADS * B_HEAD_DIM
C_KW = C_HEADS * C_DK
C_VW = C_HEADS * C_DV

LANES = 128
VMEM_LIMIT_BYTES = 56 * 1024 * 1024

NT_DIMS = (((1,), (1,)), ((), ()))
TN_DIMS = (((0,), (0,)), ((), ()))


def _rmsnorm(x, g):
    return x * lax.rsqrt(jnp.mean(x * x, axis=-1, keepdims=True) + EPS) * g


def _sigmoid(x):
    return 0.5 * jnp.tanh(0.5 * x) + 0.5


def _silu(x):
    hx = 0.5 * x
    return hx + hx * jnp.tanh(hx)


def _dot(a, b):
    return jnp.dot(a, b, preferred_element_type=F32)


def _dot_nt(a, b):
    return lax.dot_general(a, b, NT_DIMS, preferred_element_type=F32)


def _dot_tn(a, b):
    return lax.dot_general(a, b, TN_DIMS, preferred_element_type=F32)


def _params(*sem):
    return pltpu.CompilerParams(dimension_semantics=sem, vmem_limit_bytes=VMEM_LIMIT_BYTES)


def _resident(shape):
    return pl.BlockSpec(shape, lambda *_: (0,) * len(shape), pipeline_mode=pl.Buffered(1))


def _layer(stacked, layer):
    shape = stacked.shape[1:]
    return pl.BlockSpec((None,) + shape, lambda *_: (layer,) + (0,) * len(shape), pipeline_mode=pl.Buffered(1))


def _rows(tm, width):
    return pl.BlockSpec((tm, width), lambda i: (i, 0))


def _block_body(*refs, n_mix, with_ple, ga, gb, fc, n_sub):
    it = iter(refs)
    h_ref, g_ref = next(it), next(it)
    o_refs = [next(it) for _ in range(n_mix)]
    wo_ref = next(it) if n_mix else None
    win_ref, wout_ref = next(it), next(it)
    if with_ple:
        p_ref, wg_ref, wp_ref = next(it), next(it), next(it)
    out_ref, act_ref = next(it), next(it)
    d_ff = wout_ref.shape[0]
    sub = h_ref.shape[0] // n_sub

    def gain(k):
        return g_ref[k:k + 1, :]

    rows = [slice(s * sub, (s + 1) * sub) for s in range(n_sub)]
    st = [dict() for _ in range(n_sub)]

    def mix_mm(s):
        if n_mix:
            mix, lo = None, 0
            for o_ref in o_refs:
                part = _dot(o_ref[rows[s], :], wo_ref[lo:lo + o_ref.shape[1], :])
                mix = part if mix is None else mix + part
                lo += o_ref.shape[1]
            st[s]["mix"] = mix

    def pre_norm(s):
        h = h_ref[rows[s], :]
        if n_mix:
            h = h + _rmsnorm(st[s].pop("mix"), gain(3))
        st[s]["h"] = h
        st[s]["x"] = _rmsnorm(h, gain(ga)).astype(BF16)

    def hidden(s):
        x = st[s].pop("x")
        for j in range(d_ff // fc):
            gate = _dot(x, win_ref[:, j * fc:(j + 1) * fc])
            up = _dot(x, win_ref[:, d_ff + j * fc:d_ff + (j + 1) * fc])
            act_ref[rows[s], j * fc:(j + 1) * fc] = (_silu(gate) * up).astype(BF16)

    def down_mm(s):
        st[s]["y"] = _dot(act_ref[rows[s], :], wout_ref[...])

    def post_norm(s):
        st[s]["h"] = st[s]["h"] + 0.5 * _rmsnorm(st[s].pop("y"), gain(gb))

    def ple_mm(s):
        if with_ple:
            st[s]["z"] = _dot(st[s]["h"].astype(BF16), wg_ref[...])
            st[s]["proj"] = _dot(p_ref[rows[s], :].astype(BF16), wp_ref[...])

    def finish(s):
        h = st[s].pop("h")
        if with_ple:
            h = h + _rmsnorm(_sigmoid(st[s].pop("z")) * st[s].pop("proj"), gain(6))
        out_ref[rows[s], :] = h

    a, b = 0, 1
    for stage, s in [(mix_mm, a), (mix_mm, b), (pre_norm, a), (hidden, a), (pre_norm, b), (down_mm, a),
                     (hidden, b), (post_norm, a), (down_mm, b), (ple_mm, a), (post_norm, b), (finish, a),
                     (ple_mm, b), (finish, b)]:
        stage(s)


def _block(h, gains, layer, w_in, w_out, *, ga, gb, mix=None, ple=None, tm, interpret):
    t, d = h.shape
    d_ff = w_out.shape[1]
    os_, w_mix, mix_layer = mix if mix is not None else ([], None, None)
    args = [h, gains, *os_]
    specs = [_rows(tm, d), _layer(gains, layer)] + [_rows(tm, o.shape[1]) for o in os_]
    if os_:
        args.append(w_mix)
        specs.append(_layer(w_mix, mix_layer))
    args += [w_in, w_out]
    specs += [_layer(w_in, layer), _layer(w_out, layer)]
    if ple is not None:
        p_all, w_gate, w_proj = ple
        args += [p_all, w_gate, w_proj]
        specs += [pl.BlockSpec((None, tm, p_all.shape[2]), lambda i: (layer, i, 0)),
                  _layer(w_gate, layer), _layer(w_proj, layer)]
    return pl.pallas_call(
        functools.partial(_block_body, n_mix=len(os_), with_ple=ple is not None, ga=ga, gb=gb, fc=256, n_sub=2),
        grid=(t // tm,),
        in_specs=specs,
        out_specs=_rows(tm, d),
        out_shape=jax.ShapeDtypeStruct((t, d), F32),
        scratch_shapes=[pltpu.VMEM((tm, d_ff), BF16)],
        compiler_params=_params("parallel"),
        interpret=interpret, name="block",
    )(*args)


def _rope64(x, cos, sin_signed):
    n = x.shape[1]
    lane = lax.broadcasted_iota(jnp.int32, x.shape, 1)
    half = B_HEAD_DIM // 2
    swapped = jnp.where((lane % B_HEAD_DIM) < half, pltpu.roll(x, n - half, 1), pltpu.roll(x, half, 1))
    reps = n // cos.shape[1]
    cos_t = jnp.concatenate([cos] * reps, axis=1) if reps > 1 else cos
    sin_t = jnp.concatenate([sin_signed] * reps, axis=1) if reps > 1 else sin_signed
    return x * cos_t + swapped * sin_t


def _inproj_even_body(h_ref, g_ref, w_ref, lb_ref, cos_ref, sin_ref, y_ref, logf_ref):
    x = _rmsnorm(h_ref[...], g_ref[2:3, :]).astype(BF16)
    w = A_KW

    def proj(lo, hi):
        return _dot(x, w_ref[:, lo:hi])

    def put(lo, val):
        y_ref[:, lo:lo + val.shape[1]] = val.astype(y_ref.dtype)

    cos, sin = cos_ref[...], sin_ref[...]
    put(4 * w, _rope64(proj(4 * w, 4 * w + B_QW), cos, sin) * (B_HEAD_DIM ** -0.5 * LOG2E))
    kv = proj(4 * w + B_QW, 4 * w + B_QW + 2 * B_KW)
    put(4 * w + B_QW, _rope64(kv[:, :B_KW], cos, sin))
    put(4 * w + B_QW + B_KW, kv[:, B_KW:])
    lb = lb_ref[...]
    f = lb + (1.0 - lb) * _sigmoid(proj(w, 2 * w))
    logf_ref[...] = jnp.log(f)
    put(w, 1.0 - f)
    put(0, _silu(proj(0, w)))
    put(3 * w, _silu(proj(3 * w, 4 * w)))
    put(2 * w, proj(2 * w, 3 * w))


def _inproj_even(h, gains, layer, w, w_layer, lb, cos, sin, *, tm, interpret):
    t, d = h.shape
    n = w.shape[2]
    return pl.pallas_call(
        _inproj_even_body,
        grid=(t // tm,),
        in_specs=[_rows(tm, d), _layer(gains, layer), _layer(w, w_layer), _resident(lb.shape),
                  _rows(tm, LANES), _rows(tm, LANES)],
        out_specs=[_rows(tm, n), _rows(tm, A_KW)],
        out_shape=[jax.ShapeDtypeStruct((t, n), BF16), jax.ShapeDtypeStruct((t, A_KW), F32)],
        compiler_params=_params("parallel"),
        interpret=interpret, name="inproj_even",
    )(h, gains, w, lb, cos, sin)


def _inproj_odd_body(h_ref, g_ref, w_ref, cos_ref, sin_ref, rq_ref, rk_ref, y_ref):
    x = _rmsnorm(h_ref[...], g_ref[2:3, :]).astype(BF16)
    cos, sin = cos_ref[...], sin_ref[...]
    half = C_DK // 2
    reps = C_DK // LANES
    scaled0 = 2 * C_KW + 2 * C_VW

    def proj(lo, hi):
        return _dot(x, w_ref[:, lo:hi])

    def rope(v):
        x1, x2 = v[:, :half], v[:, half:]
        return jnp.concatenate([x1 * cos - x2 * sin, x2 * cos + x1 * sin], axis=-1)

    for hh in range(C_HEADS):
        for base, r_ref, scale in ((0, rq_ref, None), (C_KW, rk_ref, C_DK ** -0.5)):
            lo = base + hh * C_DK
            val = rope(proj(lo, lo + C_DK))
            if scale is not None:
                val = val * scale
            y_ref[:, lo:lo + C_DK] = val.astype(y_ref.dtype)
            y_ref[:, scaled0 + lo:scaled0 + lo + C_DK] = (
                val * jnp.concatenate([r_ref[hh]] * reps, axis=1)).astype(y_ref.dtype)
    step = 1024
    for lo in range(2 * C_KW + C_VW, 2 * C_KW + 2 * C_VW, step):
        y_ref[:, lo:lo + step] = _silu(proj(lo, lo + step)).astype(y_ref.dtype)
    for lo in range(2 * C_KW, 2 * C_KW + C_VW, step):
        y_ref[:, lo:lo + step] = proj(lo, lo + step).astype(y_ref.dtype)


def _log_gammas():
    return tuple(float(v) for v in np.log1p(-np.exp2(-5.0 - np.arange(C_HEADS, dtype=np.float64))))


def _inproj_odd(h, gains, layer, w, w_layer, cos, sin, *, tm, interpret):
    t, d = h.shape
    n = w.shape[2] + 2 * C_KW
    idx = (jnp.arange(tm, dtype=jnp.int32) % C_CHUNK).astype(F32)[None, :, None]
    lg = jnp.asarray(_log_gammas(), F32)[:, None, None]
    rq = jnp.broadcast_to(jnp.exp((idx + 1.0) * lg), (C_HEADS, tm, LANES))
    rk = jnp.broadcast_to(jnp.exp((C_CHUNK - 1.0 - idx) * lg), (C_HEADS, tm, LANES))
    return pl.pallas_call(
        _inproj_odd_body,
        grid=(t // tm,),
        in_specs=[_rows(tm, d), _layer(gains, layer), _layer(w, w_layer), _rows(tm, LANES), _rows(tm, LANES),
                  _resident(rq.shape), _resident(rk.shape)],
        out_specs=_rows(tm, n),
        out_shape=jax.ShapeDtypeStruct((t, n), BF16),
        compiler_params=_params("parallel"),
        interpret=interpret, name="inproj_odd",
    )(h, gains, w, cos, sin, rq, rk)


def _hgrn_body(q_ref, k_ref, v_ref, g_ref, lf_ref, og_ref, o_ref, st_ref, qs_ref, kd_ref, acc_ref, dec_ref):
    ts = q_ref.shape[0]
    c, grp = A_CHUNK, 2 * LANES
    per_grp = grp // c

    @pl.when(pl.program_id(1) == 0)
    def _():
        st_ref[...] = jnp.zeros_like(st_ref)

    row = lax.broadcasted_iota(jnp.int32, (grp, grp), 0)
    col = lax.broadcasted_iota(jnp.int32, (grp, grp), 1)
    causal = ((row // c) == (col // c)) & (row >= col)
    ones_tril = causal.astype(BF16)

    groups = [slice(gi * grp, (gi + 1) * grp) for gi in range(ts // grp)]
    heads = [slice(hh * A_DK, (hh + 1) * A_DK) for hh in range(A_HEADS)]
    chunks = [slice(j * c, (j + 1) * c) for j in range(ts // c)]

    bs = []
    for rows in groups:
        logf = lf_ref[rows, :]
        logf_hi = logf.astype(BF16)
        logf_lo = (logf - logf_hi.astype(F32)).astype(BF16)
        bs.append(_dot(ones_tril, logf_hi) + _dot(ones_tril, logf_lo))
    q_s, k_s = [], []
    for gi, (rows, b) in enumerate(zip(groups, bs)):
        q_s.append((q_ref[rows, :].astype(F32) * jnp.exp(b)).astype(BF16))
        k_sf = k_ref[rows, :].astype(F32) * jnp.exp(-b)
        k_s.append(k_sf.astype(BF16))
        decs = [jnp.exp(b[j * c + c - 1:j * c + c, :]) for j in range(per_grp)]
        dec_rows = jnp.concatenate([jnp.broadcast_to(dj, (c, dj.shape[1])) for dj in decs], axis=0)
        kd_ref[rows, :] = (k_sf * dec_rows).astype(BF16)
        qs_ref[rows, :] = q_s[-1]
        for j, dj in enumerate(decs):
            dec_ref[gi * per_grp + j:gi * per_grp + j + 1, :] = dj

    upd = [[_dot_tn(v_ref[rws, sl], kd_ref[rws, sl]) for rws in chunks] for sl in heads]
    att = [[jnp.where(causal, _dot_nt(q_s[gi][:, sl], k_s[gi][:, sl]), 0.0).astype(BF16) for sl in heads]
           for gi in range(len(groups))]
    for gi, rows in enumerate(groups):
        for hh, sl in enumerate(heads):
            acc_ref[rows, sl] = _dot(att[gi][hh], v_ref[rows, sl])

    seen = []
    for hh, sl in enumerate(heads):
        st = st_ref[hh]
        seen.append([])
        for j in range(len(chunks)):
            seen[hh].append(st.astype(BF16))
            st = st * dec_ref[j:j + 1, sl] + upd[hh][j]
        st_ref[hh] = st

    inter = [[_dot_nt(qs_ref[rws, sl], seen[hh][j]) for j, rws in enumerate(chunks)] for hh, sl in enumerate(heads)]
    og = og_ref[...]
    for hh, sl in enumerate(heads):
        o = acc_ref[:, sl] + jnp.concatenate(inter[hh], axis=0)
        o_ref[:, sl] = (_rmsnorm(o, og) * g_ref[:, sl].astype(F32)).astype(o_ref.dtype)


def _hgrn(y, logf, og, *, batch, seq, ts, interpret):
    ns = seq // ts
    w = A_KW

    def col(cb):
        return pl.BlockSpec((ts, w), lambda b, s: (b * ns + s, cb))

    return pl.pallas_call(
        _hgrn_body,
        grid=(batch, ns),
        in_specs=[col(0), col(1), col(2), col(3), col(0), _resident((1, A_DV))],
        out_specs=col(0),
        out_shape=jax.ShapeDtypeStruct((batch * seq, A_VW), BF16),
        scratch_shapes=[pltpu.VMEM((A_HEADS, A_DV, A_DK), F32), pltpu.VMEM((ts, w), BF16),
                        pltpu.VMEM((ts, w), BF16), pltpu.VMEM((ts, A_VW), F32),
                        pltpu.VMEM((ts // A_CHUNK, w), F32)],
        compiler_params=_params("parallel", "arbitrary"),
        interpret=interpret, name="hgrn2",
    )(y, y, y, y, logf, og)


def _swa_body(sink_ref, q_ref, kc_ref, kp_ref, vc_ref, vp_ref, o_ref, kpad_ref, vext_ref):
    blk = B_BLOCK
    tq = q_ref.shape[0]
    rep = B_Q_HEADS // B_KV_HEADS
    tiles = [(g, pos) for g in range(B_KV_HEADS) for pos in range(2)]

    kk = jnp.concatenate([kp_ref[...], kc_ref[...]], axis=0).astype(F32)
    vv = jnp.concatenate([vp_ref[...], vc_ref[...]], axis=0).astype(F32)
    low = lax.broadcasted_iota(jnp.int32, kk.shape, 1) < B_HEAD_DIM
    kk_sw = pltpu.roll(kk, B_HEAD_DIM, 1)
    vv_sw = pltpu.roll(vv, B_HEAD_DIM, 1)
    for g, pos in tiles:
        keep = low if pos == 0 else ~low
        kpad_ref[2 * g + pos] = jnp.where(keep, kk if g == pos else kk_sw, 0.0).astype(BF16)
        vext_ref[2 * g + pos, :, :B_KW] = jnp.where(keep, vv if g == pos else vv_sw, 0.0).astype(BF16)
        vext_ref[2 * g + pos, :, B_KW:] = jnp.ones((tq + blk, LANES), BF16)

    qi = lax.broadcasted_iota(jnp.int32, (2 * blk, 2 * blk), 0) % blk
    kj = lax.broadcasted_iota(jnp.int32, (2 * blk, 2 * blk), 1)
    diff = qi + blk - kj
    band = (diff >= 0) & (diff < B_WINDOW)
    bias = jnp.where(band, 0.0, -jnp.inf)
    bias_first = jnp.where(band & ((kj + (pl.program_id(1) * tq - blk)) >= 0), 0.0, -jnp.inf)
    upper = lax.broadcasted_iota(jnp.int32, (2 * blk, 1), 0) < blk
    sinks = [jnp.where(upper, sink_ref[rep * g + pos], sink_ref[rep * g + 2 + pos]) * LOG2E for g, pos in tiles]

    for n in range(tq // blk):
        rows = slice(n * blk, (n + 1) * blk)
        keys = slice(n * blk, (n + 2) * blk)
        bn = bias_first if n == 0 else bias
        qs = [jnp.concatenate([q_ref[rows, j * LANES:(j + 1) * LANES] for j in (2 * g, 2 * g + 1)], axis=0)
              for g in range(B_KV_HEADS)]
        s = [_dot_nt(qs[g], kpad_ref[2 * g + pos, keys, :]) + bn for g, pos in tiles]
        m = [jnp.max(si, axis=-1, keepdims=True) for si in s]
        pr = [jnp.exp2(si - mi).astype(BF16) for si, mi in zip(s, m)]
        pv = [_dot(pi, vext_ref[2 * g + pos, keys, :]) for pi, (g, pos) in zip(pr, tiles)]
        res = [pvi[:, :LANES] / (pvi[:, LANES:] + jnp.exp2(sk - mi)) for pvi, sk, mi in zip(pv, sinks, m)]
        for j in range(B_QW // LANES):
            g, i = j // 2, j % 2
            oj = res[2 * g][i * blk:(i + 1) * blk] + res[2 * g + 1][i * blk:(i + 1) * blk]
            o_ref[rows, j * LANES:(j + 1) * LANES] = oj.astype(o_ref.dtype)


def _swa(y, sinks, *, batch, seq, tq, q_cb, k_cb, v_cb, interpret):
    blk = B_BLOCK
    nt = seq // tq
    per = tq // blk

    def cur(width, cb):
        return pl.BlockSpec((tq, width), lambda b, n: (b * nt + n, cb))

    def prev(width, cb):
        return pl.BlockSpec((blk, width), lambda b, n: (jnp.maximum((b * nt + n) * per - 1, 0), cb))

    return pl.pallas_call(
        _swa_body,
        grid=(batch, nt),
        in_specs=[pl.BlockSpec(memory_space=pltpu.SMEM),
                  cur(B_QW, q_cb), cur(B_KW, k_cb), prev(B_KW, k_cb), cur(B_KW, v_cb), prev(B_KW, v_cb)],
        out_specs=cur(B_QW, 0),
        out_shape=jax.ShapeDtypeStruct((batch * seq, B_QW), BF16),
        scratch_shapes=[pltpu.VMEM((4, tq + blk, B_KW), BF16), pltpu.VMEM((4, tq + blk, 2 * LANES), BF16)],
        compiler_params=_params("parallel", "arbitrary"),
        interpret=interpret, name="swa",
    )(sinks, y, y, y, y, y)


def _retention_body(q_ref, k_ref, v_ref, g_ref, qi_ref, ki_ref, og_ref, o_ref, st_ref, decay_ref, *, log_gammas):
    c = C_CHUNK

    @pl.when(pl.program_id(1) == 0)
    def _():
        st_ref[...] = jnp.zeros_like(st_ref)
        row = lax.broadcasted_iota(jnp.int32, (c, c), 0)
        col = lax.broadcasted_iota(jnp.int32, (c, c), 1)
        diff = (row - col).astype(F32)
        for hh, lg in enumerate(log_gammas):
            decay_ref[hh] = jnp.where(diff >= 0, jnp.exp(jnp.maximum(diff, 0.0) * lg), 0.0)

    nh = len(log_gammas)
    ksl = [slice(hh * C_DK, (hh + 1) * C_DK) for hh in range(nh)]
    vsl = [slice(hh * C_DV, (hh + 1) * C_DV) for hh in range(nh)]
    chunks = [slice(j * c, (j + 1) * c) for j in range(q_ref.shape[0] // c)]
    att = [[(_dot_nt(q_ref[rows, ksl[hh]], k_ref[rows, ksl[hh]]) * decay_ref[hh]).astype(BF16) for hh in range(nh)]
           for rows in chunks]
    upd = [[_dot_tn(ki_ref[rows, ksl[hh]], v_ref[rows, vsl[hh]]) for hh in range(nh)] for rows in chunks]
    st = [st_ref[hh] for hh in range(nh)]
    o = []
    for j, rows in enumerate(chunks):
        o.append([_dot(att[j][hh], v_ref[rows, vsl[hh]]) + _dot(qi_ref[rows, ksl[hh]], st[hh].astype(BF16))
                  for hh in range(nh)])
        st = [st[hh] * float(np.exp(c * lg)) + upd[j][hh] for hh, lg in enumerate(log_gammas)]
    for hh in range(nh):
        st_ref[hh] = st[hh]
    og = og_ref[...]
    for j, rows in enumerate(chunks):
        for hh in range(nh):
            o_ref[rows, vsl[hh]] = _rmsnorm(o[j][hh], og).astype(o_ref.dtype) * g_ref[rows, vsl[hh]]


def _retention(y, og, *, batch, seq, ts, interpret):
    c = C_CHUNK
    ns = seq // ts
    scaled_cb = (2 * C_KW + 2 * C_VW) // C_KW

    def blk(width, cb):
        return pl.BlockSpec((ts, width), lambda b, s: (b * ns + s, cb))

    return pl.pallas_call(
        functools.partial(_retention_body, log_gammas=_log_gammas()),
        grid=(batch, ns),
        in_specs=[blk(C_KW, 0), blk(C_KW, 1), blk(C_VW, 1), blk(C_VW, 2), blk(C_KW, scaled_cb),
                  blk(C_KW, scaled_cb + 1), _resident((1, C_DV))],
        out_specs=blk(C_VW, 0),
        out_shape=jax.ShapeDtypeStruct((batch * seq, C_VW), BF16),
        scratch_shapes=[pltpu.VMEM((C_HEADS, C_DK, C_DV), F32), pltpu.VMEM((C_HEADS, c, c), F32)],
        compiler_params=_params("parallel", "arbitrary"),
        interpret=interpret, name="retention",
    )(y, y, y, y, y, y, og)


def _forward(x, p, positions, norm_g, ffn1_w_in, ffn1_w_out, ffn2_w_in, ffn2_w_out, ple_w_proj, ple_w_gate,
             even_w_in, even_w_out, hgrn_lb, hgrn_onorm_g, attn_sinks, odd_w_in, odd_w_out, ret_onorm_g,
             *, interpret=False):
    batch, seq, d = x.shape
    t = batch * seq
    depth = norm_g.shape[0]
    tm = min(512, t)
    kw = dict(interpret=interpret)

    pos = positions.astype(F32).reshape(t, 1)
    inv_b = ROPE_THETA ** (-jnp.arange(0, B_HEAD_DIM, 2, dtype=F32) / B_HEAD_DIM)
    inv_c = RET_THETA ** (-jnp.linspace(0.0, 1.0, C_DK // 2, dtype=F32))
    ang_b = pos * inv_b[None, :]
    ang_c = pos * inv_c[None, :]
    cos_b = jnp.tile(jnp.cos(ang_b), (1, 4))
    sin_b = jnp.tile(jnp.concatenate([-jnp.sin(ang_b), jnp.sin(ang_b)], axis=1), (1, 2))
    cos_c, sin_c = jnp.cos(ang_c), jnp.sin(ang_c)
    lb_all = jnp.cumsum(jax.nn.softmax(hgrn_lb.astype(F32), axis=0), axis=0)
    p_all = p.reshape(depth, t, p.shape[-1])

    w1_in, w1_out, w2_in, w2_out = (w.astype(BF16) for w in (ffn1_w_in, ffn1_w_out, ffn2_w_in, ffn2_w_out))
    wp_gate, wp_proj = ple_w_gate.astype(BF16), ple_w_proj.astype(BF16)
    we_in, we_out, wo_in, wo_out = (w.astype(BF16) for w in (even_w_in, even_w_out, odd_w_in, odd_w_out))

    h = x.reshape(t, d)
    for i in range(depth):
        j = i // 2
        h = _block(h, norm_g, i, w1_in, w1_out, ga=0, gb=1, tm=min(1024, t), **kw)
        if i % 2 == 0:
            y, logf = _inproj_even(h, norm_g, i, we_in, j, lb_all[j][None, :], cos_b, sin_b, tm=min(1024, t), **kw)
            oa = _hgrn(y, logf, hgrn_onorm_g[j][None, :], batch=batch, seq=seq, ts=min(1024, seq), **kw)
            q_cb = (2 * A_KW + 2 * A_VW) // B_QW
            k_cb = (2 * A_KW + 2 * A_VW + B_QW) // B_KW
            ob = _swa(y, attn_sinks[j].astype(F32), batch=batch, seq=seq, tq=min(1024, seq),
                      q_cb=q_cb, k_cb=k_cb, v_cb=k_cb + 1, **kw)
            mix = ([oa, ob], we_out, j)
        else:
            y = _inproj_odd(h, norm_g, i, wo_in, j, cos_c, sin_c, tm=tm, **kw)
            o = _retention(y, ret_onorm_g[j][None, :], batch=batch, seq=seq, ts=min(2 * C_CHUNK, seq), **kw)
            mix = ([o], wo_out, j)
        h = _block(h, norm_g, i, w2_in, w2_out, ga=4, gb=5, mix=mix, ple=(p_all, wp_gate, wp_proj), tm=tm, **kw)
    return h.reshape(batch, seq, d)


def kernel(x, p, positions, norm_g, ffn1_w_in, ffn1_w_out, ffn2_w_in, ffn2_w_out, ple_w_proj, ple_w_gate,
           even_w_in, even_w_out, hgrn_lb, hgrn_onorm_g, attn_sinks, odd_w_in, odd_w_out, ret_onorm_g):
    return _forward(x, p, positions, norm_g, ffn1_w_in, ffn1_w_out, ffn2_w_in, ffn2_w_out, ple_w_proj, ple_w_gate,
                    even_w_in, even_w_out, hgrn_lb, hgrn_onorm_g, attn_sinks, odd_w_in, odd_w_out, ret_onorm_g)
```

```python
import functools

import numpy as np
import jax
import jax.numpy as jnp
from jax import lax
from jax.experimental import pallas as pl
from jax.experimental.pallas import tpu as pltpu

F32 = jnp.float32
BF16 = jnp.bfloat16

EPS = 1e-6
ROPE_THETA = 10000.0
RET_THETA = 10000.0
LOG2E = 1.4426950408889634

A_HEADS, A_DK, A_DV, A_CHUNK = 4, 128, 128, 64
B_Q_HEADS, B_KV_HEADS, B_HEAD_DIM, B_WINDOW, B_BLOCK = 8, 2, 64, 128, 128
C_HEADS, C_DK, C_DV = 4, 256, 512
C_CHUNK = 256

A_KW = A_HEADS * A_DK
A_VW = A_HEADS * A_DV
B_QW = B_Q_HEADS * B_HEAD_DIM
B_KW = B_KV_HEADS * B_HEAD_DIM
C_KW = C_HEADS * C_DK
C_VW = C_HEADS * C_DV

LANES = 128
VMEM_LIMIT_BYTES = 56 * 1024 * 1024

NT_DIMS = (((1,), (1,)), ((), ()))
TN_DIMS = (((0,), (0,)), ((), ()))


def _rmsnorm(x, g):
    return x * lax.rsqrt(jnp.mean(x * x, axis=-1, keepdims=True) + EPS) * g


def _sigmoid(x):
    return 0.5 * jnp.tanh(0.5 * x) + 0.5


def _silu(x):
    hx = 0.5 * x
    return hx + hx * jnp.tanh(hx)


def _dot(a, b):
    return jnp.dot(a, b, preferred_element_type=F32)


def _dot_nt(a, b):
    return lax.dot_general(a, b, NT_DIMS, preferred_element_type=F32)


def _dot_tn(a, b):
    return lax.dot_general(a, b, TN_DIMS, preferred_element_type=F32)


def _params(*sem):
    return pltpu.CompilerParams(dimension_semantics=sem, vmem_limit_bytes=VMEM_LIMIT_BYTES)


def _resident(shape):
    return pl.BlockSpec(shape, lambda *_: (0,) * len(shape), pipeline_mode=pl.Buffered(1))


def _layer(stacked, layer):
    shape = stacked.shape[1:]
    return pl.BlockSpec((None,) + shape, lambda *_: (layer,) + (0,) * len(shape), pipeline_mode=pl.Buffered(1))


def _rows(tm, width):
    return pl.BlockSpec((tm, width), lambda i: (i, 0))


def _block_body(*refs, n_mix, with_ple, ga, gb, fc, n_sub):
    it = iter(refs)
    h_ref, g_ref = next(it), next(it)
    o_refs = [next(it) for _ in range(n_mix)]
    wo_ref = next(it) if n_mix else None
    win_ref, wout_ref = next(it), next(it)
    if with_ple:
        p_ref, wg_ref, wp_ref = next(it), next(it), next(it)
    out_ref, act_ref = next(it), next(it)
    d_ff = wout_ref.shape[0]
    sub = h_ref.shape[0] // n_sub

    def gain(k):
        return g_ref[k:k + 1, :]

    rows = [slice(s * sub, (s + 1) * sub) for s in range(n_sub)]
    st = [dict() for _ in range(n_sub)]

    def mix_mm(s):
        if n_mix:
            mix, lo = None, 0
            for o_ref in o_refs:
                part = _dot(o_ref[rows[s], :], wo_ref[lo:lo + o_ref.shape[1], :])
                mix = part if mix is None else mix + part
                lo += o_ref.shape[1]
            st[s]["mix"] = mix

    def pre_norm(s):
        h = h_ref[rows[s], :]
        if n_mix:
            h = h + _rmsnorm(st[s].pop("mix"), gain(3))
        st[s]["h"] = h
        st[s]["x"] = _rmsnorm(h, gain(ga)).astype(BF16)

    def hidden(s):
        x = st[s].pop("x")
        for j in range(d_ff // fc):
            gate = _dot(x, win_ref[:, j * fc:(j + 1) * fc])
            up = _dot(x, win_ref[:, d_ff + j * fc:d_ff + (j + 1) * fc])
            act_ref[rows[s], j * fc:(j + 1) * fc] = (_silu(gate) * up).astype(BF16)

    def down_mm(s):
        st[s]["y"] = _dot(act_ref[rows[s], :], wout_ref[...])

    def post_norm(s):
        st[s]["h"] = st[s]["h"] + 0.5 * _rmsnorm(st[s].pop("y"), gain(gb))

    def ple_mm(s):
        if with_ple:
            st[s]["z"] = _dot(st[s]["h"].astype(BF16), wg_ref[...])
            st[s]["proj"] = _dot(p_ref[rows[s], :].astype(BF16), wp_ref[...])

    def finish(s):
        h = st[s].pop("h")
        if with_ple:
            h = h + _rmsnorm(_sigmoid(st[s].pop("z")) * st[s].pop("proj"), gain(6))
        out_ref[rows[s], :] = h

    a, b = 0, 1
    for stage, s in [(mix_mm, a), (mix_mm, b), (pre_norm, a), (hidden, a), (pre_norm, b), (down_mm, a),
                     (hidden, b), (post_norm, a), (down_mm, b), (ple_mm, a), (post_norm, b), (finish, a),
                     (ple_mm, b), (finish, b)]:
        stage(s)


def _block(h, gains, layer, w_in, w_out, *, ga, gb, mix=None, ple=None, tm, interpret):
    t, d = h.shape
    d_ff = w_out.shape[1]
    os_, w_mix, mix_layer = mix if mix is not None else ([], None, None)
    args = [h, gains, *os_]
    specs = [_rows(tm, d), _layer(gains, layer)] + [_rows(tm, o.shape[1]) for o in os_]
    if os_:
        args.append(w_mix)
        specs.append(_layer(w_mix, mix_layer))
    args += [w_in, w_out]
    specs += [_layer(w_in, layer), _layer(w_out, layer)]
    if ple is not None:
        p_all, w_gate, w_proj = ple
        args += [p_all, w_gate, w_proj]
        specs += [pl.BlockSpec((None, tm, p_all.shape[2]), lambda i: (layer, i, 0)),
                  _layer(w_gate, layer), _layer(w_proj, layer)]
    return pl.pallas_call(
        functools.partial(_block_body, n_mix=len(os_), with_ple=ple is not None, ga=ga, gb=gb, fc=256, n_sub=2),
        grid=(t // tm,),
        in_specs=specs,
        out_specs=_rows(tm, d),
        out_shape=jax.ShapeDtypeStruct((t, d), F32),
        scratch_shapes=[pltpu.VMEM((tm, d_ff), BF16)],
        compiler_params=_params("parallel"),
        interpret=interpret, name="block",
    )(*args)


def _rope64(x, cos, sin_signed):
    n = x.shape[1]
    lane = lax.broadcasted_iota(jnp.int32, x.shape, 1)
    half = B_HEAD_DIM // 2
    swapped = jnp.where((lane % B_HEAD_DIM) < half, pltpu.roll(x, n - half, 1), pltpu.roll(x, half, 1))
    reps = n // cos.shape[1]
    cos_t = jnp.concatenate([cos] * reps, axis=1) if reps > 1 else cos
    sin_t = jnp.concatenate([sin_signed] * reps, axis=1) if reps > 1 else sin_signed
    return x * cos_t + swapped * sin_t


def _inproj_even_body(h_ref, g_ref, w_ref, lb_ref, cos_ref, sin_ref, y_ref, logf_ref):
    x = _rmsnorm(h_ref[...], g_ref[2:3, :]).astype(BF16)
    w = A_KW

    def proj(lo, hi):
        return _dot(x, w_ref[:, lo:hi])

    def put(lo, val):
        y_ref[:, lo:lo + val.shape[1]] = val.astype(y_ref.dtype)

    cos, sin = cos_ref[...], sin_ref[...]
    put(4 * w, _rope64(proj(4 * w, 4 * w + B_QW), cos, sin) * (B_HEAD_DIM ** -0.5 * LOG2E))
    kv = proj(4 * w + B_QW, 4 * w + B_QW + 2 * B_KW)
    put(4 * w + B_QW, _rope64(kv[:, :B_KW], cos, sin))
    put(4 * w + B_QW + B_KW, kv[:, B_KW:])
    lb = lb_ref[...]
    f = lb + (1.0 - lb) * _sigmoid(proj(w, 2 * w))
    logf_ref[...] = jnp.log(f)
    put(w, 1.0 - f)
    put(0, _silu(proj(0, w)))
    put(3 * w, _silu(proj(3 * w, 4 * w)))
    put(2 * w, proj(2 * w, 3 * w))


def _inproj_even(h, gains, layer, w, w_layer, lb, cos, sin, *, tm, interpret):
    t, d = h.shape
    n = w.shape[2]
    return pl.pallas_call(
        _inproj_even_body,
        grid=(t // tm,),
        in_specs=[_rows(tm, d), _layer(gains, layer), _layer(w, w_layer), _resident(lb.shape),
                  _rows(tm, LANES), _rows(tm, LANES)],
        out_specs=[_rows(tm, n), _rows(tm, A_KW)],
        out_shape=[jax.ShapeDtypeStruct((t, n), BF16), jax.ShapeDtypeStruct((t, A_KW), F32)],
        compiler_params=_params("parallel"),
        interpret=interpret, name="inproj_even",
    )(h, gains, w, lb, cos, sin)


def _inproj_odd_body(h_ref, g_ref, w_ref, cos_ref, sin_ref, y_ref):
    x = _rmsnorm(h_ref[...], g_ref[2:3, :]).astype(BF16)
    cos, sin = cos_ref[...], sin_ref[...]
    half = C_DK // 2

    def proj(lo, hi):
        return _dot(x, w_ref[:, lo:hi])

    def rope(v):
        x1, x2 = v[:, :half], v[:, half:]
        return jnp.concatenate([x1 * cos - x2 * sin, x2 * cos + x1 * sin], axis=-1)

    for hh in range(C_HEADS):
        lo = hh * C_DK
        y_ref[:, lo:lo + C_DK] = rope(proj(lo, lo + C_DK)).astype(y_ref.dtype)
        lo = C_KW + hh * C_DK
        y_ref[:, lo:lo + C_DK] = (rope(proj(lo, lo + C_DK)) * (C_DK ** -0.5)).astype(y_ref.dtype)
    step = 1024
    for lo in range(2 * C_KW + C_VW, 2 * C_KW + 2 * C_VW, step):
        y_ref[:, lo:lo + step] = _silu(proj(lo, lo + step)).astype(y_ref.dtype)
    for lo in range(2 * C_KW, 2 * C_KW + C_VW, step):
        y_ref[:, lo:lo + step] = proj(lo, lo + step).astype(y_ref.dtype)


def _inproj_odd(h, gains, layer, w, w_layer, cos, sin, *, tm, interpret):
    t, d = h.shape
    n = w.shape[2]
    return pl.pallas_call(
        _inproj_odd_body,
        grid=(t // tm,),
        in_specs=[_rows(tm, d), _layer(gains, layer), _layer(w, w_layer), _rows(tm, LANES), _rows(tm, LANES)],
        out_specs=_rows(tm, n),
        out_shape=jax.ShapeDtypeStruct((t, n), BF16),
        compiler_params=_params("parallel"),
        interpret=interpret, name="inproj_odd",
    )(h, gains, w, cos, sin)


def _hgrn_body(q_ref, k_ref, v_ref, g_ref, lf_ref, og_ref, o_ref, st_ref, qs_ref, kd_ref, acc_ref, dec_ref):
    ts = q_ref.shape[0]
    c, grp = A_CHUNK, 2 * LANES
    per_grp = grp // c

    @pl.when(pl.program_id(1) == 0)
    def _():
        st_ref[...] = jnp.zeros_like(st_ref)

    row = lax.broadcasted_iota(jnp.int32, (grp, grp), 0)
    col = lax.broadcasted_iota(jnp.int32, (grp, grp), 1)
    causal = ((row // c) == (col // c)) & (row >= col)
    ones_tril = causal.astype(BF16)

    groups = [slice(gi * grp, (gi + 1) * grp) for gi in range(ts // grp)]
    heads = [slice(hh * A_DK, (hh + 1) * A_DK) for hh in range(A_HEADS)]
    chunks = [slice(j * c, (j + 1) * c) for j in range(ts // c)]

    bs = []
    for rows in groups:
        logf = lf_ref[rows, :]
        logf_hi = logf.astype(BF16)
        logf_lo = (logf - logf_hi.astype(F32)).astype(BF16)
        bs.append(_dot(ones_tril, logf_hi) + _dot(ones_tril, logf_lo))
    q_s, k_s = [], []
    for gi, (rows, b) in enumerate(zip(groups, bs)):
        q_s.append((q_ref[rows, :].astype(F32) * jnp.exp(b)).astype(BF16))
        k_sf = k_ref[rows, :].astype(F32) * jnp.exp(-b)
        k_s.append(k_sf.astype(BF16))
        decs = [jnp.exp(b[j * c + c - 1:j * c + c, :]) for j in range(per_grp)]
        dec_rows = jnp.concatenate([jnp.broadcast_to(dj, (c, dj.shape[1])) for dj in decs], axis=0)
        kd_ref[rows, :] = (k_sf * dec_rows).astype(BF16)
        qs_ref[rows, :] = q_s[-1]
        for j, dj in enumerate(decs):
            dec_ref[gi * per_grp + j:gi * per_grp + j + 1, :] = dj

    upd = [[_dot_tn(v_ref[rws, sl], kd_ref[rws, sl]) for rws in chunks] for sl in heads]
    att = [[jnp.where(causal, _dot_nt(q_s[gi][:, sl], k_s[gi][:, sl]), 0.0).astype(BF16) for sl in heads]
           for gi in range(len(groups))]
    for gi, rows in enumerate(groups):
        for hh, sl in enumerate(heads):
            acc_ref[rows, sl] = _dot(att[gi][hh], v_ref[rows, sl])

    seen = []
    for hh, sl in enumerate(heads):
        st = st_ref[hh]
        seen.append([])
        for j in range(len(chunks)):
            seen[hh].append(st.astype(BF16))
            st = st * dec_ref[j:j + 1, sl] + upd[hh][j]
        st_ref[hh] = st

    inter = [[_dot_nt(qs_ref[rws, sl], seen[hh][j]) for j, rws in enumerate(chunks)] for hh, sl in enumerate(heads)]
    og = og_ref[...]
    for hh, sl in enumerate(heads):
        o = acc_ref[:, sl] + jnp.concatenate(inter[hh], axis=0)
        o_ref[:, sl] = (_rmsnorm(o, og) * g_ref[:, sl].astype(F32)).astype(o_ref.dtype)


def _hgrn(y, logf, og, *, batch, seq, ts, interpret):
    ns = seq // ts
    w = A_KW

    def col(cb):
        return pl.BlockSpec((ts, w), lambda b, s: (b * ns + s, cb))

    return pl.pallas_call(
        _hgrn_body,
        grid=(batch, ns),
        in_specs=[col(0), col(1), col(2), col(3), col(0), _resident((1, A_DV))],
        out_specs=col(0),
        out_shape=jax.ShapeDtypeStruct((batch * seq, A_VW), BF16),
        scratch_shapes=[pltpu.VMEM((A_HEADS, A_DV, A_DK), F32), pltpu.VMEM((ts, w), BF16),
                        pltpu.VMEM((ts, w), BF16), pltpu.VMEM((ts, A_VW), F32),
                        pltpu.VMEM((ts // A_CHUNK, w), F32)],
        compiler_params=_params("parallel", "arbitrary"),
        interpret=interpret, name="hgrn2",
    )(y, y, y, y, logf, og)


def _swa_body(sink_ref, q_ref, kc_ref, kp_ref, vc_ref, vp_ref, o_ref, kpad_ref, vext_ref):
    blk = B_BLOCK
    tq = q_ref.shape[0]
    rep = B_Q_HEADS // B_KV_HEADS
    tiles = [(g, pos) for g in range(B_KV_HEADS) for pos in range(2)]

    kk = jnp.concatenate([kp_ref[...], kc_ref[...]], axis=0).astype(F32)
    vv = jnp.concatenate([vp_ref[...], vc_ref[...]], axis=0).astype(F32)
    low = lax.broadcasted_iota(jnp.int32, kk.shape, 1) < B_HEAD_DIM
    kk_sw = pltpu.roll(kk, B_HEAD_DIM, 1)
    vv_sw = pltpu.roll(vv, B_HEAD_DIM, 1)
    for g, pos in tiles:
        keep = low if pos == 0 else ~low
        kpad_ref[2 * g + pos] = jnp.where(keep, kk if g == pos else kk_sw, 0.0).astype(BF16)
        vext_ref[2 * g + pos, :, :B_KW] = jnp.where(keep, vv if g == pos else vv_sw, 0.0).astype(BF16)
        vext_ref[2 * g + pos, :, B_KW:] = jnp.ones((tq + blk, LANES), BF16)

    qi = lax.broadcasted_iota(jnp.int32, (2 * blk, 2 * blk), 0) % blk
    kj = lax.broadcasted_iota(jnp.int32, (2 * blk, 2 * blk), 1)
    diff = qi + blk - kj
    band = (diff >= 0) & (diff < B_WINDOW)
    bias = jnp.where(band, 0.0, -jnp.inf)
    bias_first = jnp.where(band & ((kj + (pl.program_id(1) * tq - blk)) >= 0), 0.0, -jnp.inf)
    upper = lax.broadcasted_iota(jnp.int32, (2 * blk, 1), 0) < blk
    sinks = [jnp.where(upper, sink_ref[rep * g + pos], sink_ref[rep * g + 2 + pos]) * LOG2E for g, pos in tiles]

    for n in range(tq // blk):
        rows = slice(n * blk, (n + 1) * blk)
        keys = slice(n * blk, (n + 2) * blk)
        bn = bias_first if n == 0 else bias
        qs = [jnp.concatenate([q_ref[rows, j * LANES:(j + 1) * LANES] for j in (2 * g, 2 * g + 1)], axis=0)
              for g in range(B_KV_HEADS)]
        s = [_dot_nt(qs[g], kpad_ref[2 * g + pos, keys, :]) + bn for g, pos in tiles]
        m = [jnp.max(si, axis=-1, keepdims=True) for si in s]
        pr = [jnp.exp2(si - mi).astype(BF16) for si, mi in zip(s, m)]
        pv = [_dot(pi, vext_ref[2 * g + pos, keys, :]) for pi, (g, pos) in zip(pr, tiles)]
        res = [pvi[:, :LANES] / (pvi[:, LANES:] + jnp.exp2(sk - mi)) for pvi, sk, mi in zip(pv, sinks, m)]
        for j in range(B_QW // LANES):
            g, i = j // 2, j % 2
            oj = res[2 * g][i * blk:(i + 1) * blk] + res[2 * g + 1][i * blk:(i + 1) * blk]
            o_ref[rows, j * LANES:(j + 1) * LANES] = oj.astype(o_ref.dtype)


def _swa(y, sinks, *, batch, seq, tq, q_cb, k_cb, v_cb, interpret):
    blk = B_BLOCK
    nt = seq // tq
    per = tq // blk

    def cur(width, cb):
        return pl.BlockSpec((tq, width), lambda b, n: (b * nt + n, cb))

    def prev(width, cb):
        return pl.BlockSpec((blk, width), lambda b, n: (jnp.maximum((b * nt + n) * per - 1, 0), cb))

    return pl.pallas_call(
        _swa_body,
        grid=(batch, nt),
        in_specs=[pl.BlockSpec(memory_space=pltpu.SMEM),
                  cur(B_QW, q_cb), cur(B_KW, k_cb), prev(B_KW, k_cb), cur(B_KW, v_cb), prev(B_KW, v_cb)],
        out_specs=cur(B_QW, 0),
        out_shape=jax.ShapeDtypeStruct((batch * seq, B_QW), BF16),
        scratch_shapes=[pltpu.VMEM((4, tq + blk, B_KW), BF16), pltpu.VMEM((4, tq + blk, 2 * LANES), BF16)],
        compiler_params=_params("parallel", "arbitrary"),
        interpret=interpret, name="swa",
    )(sinks, y, y, y, y, y)


def _retention_body(q_ref, k_ref, v_ref, g_ref, og_ref, o_ref, st_ref, decay_ref, rq_ref, rk_ref, *, log_gammas):
    c = C_CHUNK

    @pl.when(pl.program_id(1) == 0)
    def _():
        st_ref[...] = jnp.zeros_like(st_ref)
        row = lax.broadcasted_iota(jnp.int32, (c, c), 0)
        col = lax.broadcasted_iota(jnp.int32, (c, c), 1)
        diff = (row - col).astype(F32)
        rowf = row[:, :LANES].astype(F32)
        for hh, lg in enumerate(log_gammas):
            decay_ref[hh] = jnp.where(diff >= 0, jnp.exp(jnp.maximum(diff, 0.0) * lg), 0.0)
            rq_ref[hh] = jnp.exp((rowf + 1.0) * lg)
            rk_ref[hh] = jnp.exp((c - 1.0 - rowf) * lg)

    nh = len(log_gammas)
    ksl = [slice(hh * C_DK, (hh + 1) * C_DK) for hh in range(nh)]
    vsl = [slice(hh * C_DV, (hh + 1) * C_DV) for hh in range(nh)]
    chunks = [slice(j * c, (j + 1) * c) for j in range(q_ref.shape[0] // c)]
    reps = C_DK // LANES
    att = [[(_dot_nt(q_ref[rows, ksl[hh]], k_ref[rows, ksl[hh]]) * decay_ref[hh]).astype(BF16) for hh in range(nh)]
           for rows in chunks]
    q_in = [[(q_ref[rows, ksl[hh]].astype(F32) * jnp.concatenate([rq_ref[hh]] * reps, axis=1)).astype(BF16)
             for hh in range(nh)] for rows in chunks]
    k_in = [[(k_ref[rows, ksl[hh]].astype(F32) * jnp.concatenate([rk_ref[hh]] * reps, axis=1)).astype(BF16)
             for hh in range(nh)] for rows in chunks]
    upd = [[_dot_tn(k_in[j][hh], v_ref[rows, vsl[hh]]) for hh in range(nh)] for j, rows in enumerate(chunks)]
    st = [st_ref[hh] for hh in range(nh)]
    o = []
    for j, rows in enumerate(chunks):
        o.append([_dot(att[j][hh], v_ref[rows, vsl[hh]]) + _dot(q_in[j][hh], st[hh].astype(BF16))
                  for hh in range(nh)])
        st = [st[hh] * float(np.exp(c * lg)) + upd[j][hh] for hh, lg in enumerate(log_gammas)]
    for hh in range(nh):
        st_ref[hh] = st[hh]
    og = og_ref[...]
    for j, rows in enumerate(chunks):
        for hh in range(nh):
            gate = g_ref[rows, vsl[hh]].astype(F32)
            o_ref[rows, vsl[hh]] = (_rmsnorm(o[j][hh], og) * gate).astype(o_ref.dtype)


def _retention(y, og, *, batch, seq, ts, interpret):
    c = C_CHUNK
    ns = seq // ts
    log_gammas = tuple(float(v) for v in np.log1p(-np.exp2(-5.0 - np.arange(C_HEADS, dtype=np.float64))))

    def blk(width, cb):
        return pl.BlockSpec((ts, width), lambda b, s: (b * ns + s, cb))

    return pl.pallas_call(
        functools.partial(_retention_body, log_gammas=log_gammas),
        grid=(batch, ns),
        in_specs=[blk(C_KW, 0), blk(C_KW, 1), blk(C_VW, 1), blk(C_VW, 2), _resident((1, C_DV))],
        out_specs=blk(C_VW, 0),
        out_shape=jax.ShapeDtypeStruct((batch * seq, C_VW), BF16),
        scratch_shapes=[pltpu.VMEM((C_HEADS, C_DK, C_DV), F32), pltpu.VMEM((C_HEADS, c, c), F32),
                        pltpu.VMEM((C_HEADS, c, LANES), F32), pltpu.VMEM((C_HEADS, c, LANES), F32)],
        compiler_params=_params("parallel", "arbitrary"),
        interpret=interpret, name="retention",
    )(y, y, y, y, og)


def _forward(x, p, positions, norm_g, ffn1_w_in, ffn1_w_out, ffn2_w_in, ffn2_w_out, ple_w_proj, ple_w_gate,
             even_w_in, even_w_out, hgrn_lb, hgrn_onorm_g, attn_sinks, odd_w_in, odd_w_out, ret_onorm_g,
             *, interpret=False):
    batch, seq, d = x.shape
    t = batch * seq
    depth = norm_g.shape[0]
    tm = min(512, t)
    kw = dict(interpret=interpret)

    pos = positions.astype(F32).reshape(t, 1)
    inv_b = ROPE_THETA ** (-jnp.arange(0, B_HEAD_DIM, 2, dtype=F32) / B_HEAD_DIM)
    inv_c = RET_THETA ** (-jnp.linspace(0.0, 1.0, C_DK // 2, dtype=F32))
    ang_b = pos * inv_b[None, :]
    ang_c = pos * inv_c[None, :]
    cos_b = jnp.tile(jnp.cos(ang_b), (1, 4))
    sin_b = jnp.tile(jnp.concatenate([-jnp.sin(ang_b), jnp.sin(ang_b)], axis=1), (1, 2))
    cos_c, sin_c = jnp.cos(ang_c), jnp.sin(ang_c)
    lb_all = jnp.cumsum(jax.nn.softmax(hgrn_lb.astype(F32), axis=0), axis=0)
    p_all = p.reshape(depth, t, p.shape[-1])

    w1_in, w1_out, w2_in, w2_out = (w.astype(BF16) for w in (ffn1_w_in, ffn1_w_out, ffn2_w_in, ffn2_w_out))
    wp_gate, wp_proj = ple_w_gate.astype(BF16), ple_w_proj.astype(BF16)
    we_in, we_out, wo_in, wo_out = (w.astype(BF16) for w in (even_w_in, even_w_out, odd_w_in, odd_w_out))

    h = x.reshape(t, d)
    for i in range(depth):
        j = i // 2
        h = _block(h, norm_g, i, w1_in, w1_out, ga=0, gb=1, tm=min(1024, t), **kw)
        if i % 2 == 0:
            y, logf = _inproj_even(h, norm_g, i, we_in, j, lb_all[j][None, :], cos_b, sin_b, tm=min(1024, t), **kw)
            oa = _hgrn(y, logf, hgrn_onorm_g[j][None, :], batch=batch, seq=seq, ts=min(1024, seq), **kw)
            q_cb = (2 * A_KW + 2 * A_VW) // B_QW
            k_cb = (2 * A_KW + 2 * A_VW + B_QW) // B_KW
            ob = _swa(y, attn_sinks[j].astype(F32), batch=batch, seq=seq, tq=min(1024, seq),
                      q_cb=q_cb, k_cb=k_cb, v_cb=k_cb + 1, **kw)
            mix = ([oa, ob], we_out, j)
        else:
            y = _inproj_odd(h, norm_g, i, wo_in, j, cos_c, sin_c, tm=min(1024, t), **kw)
            o = _retention(y, ret_onorm_g[j][None, :], batch=batch, seq=seq, ts=min(4 * C_CHUNK, seq), **kw)
            mix = ([o], wo_out, j)
        h = _block(h, norm_g, i, w2_in, w2_out, ga=4, gb=5, mix=mix, ple=(p_all, wp_gate, wp_proj), tm=tm, **kw)
    return h.reshape(batch, seq, d)


def kernel(x, p, positions, norm_g, ffn1_w_in, ffn1_w_out, ffn2_w_in, ffn2_w_out, ple_w_proj, ple_w_gate,
           even_w_in, even_w_out, hgrn_lb, hgrn_onorm_g, attn_sinks, odd_w_in, odd_w_out, ret_onorm_g):
    return _forward(x, p, positions, norm_g, ffn1_w_in, ffn1_w_out, ffn2_w_in, ffn2_w_out, ple_w_proj, ple_w_gate,
                    even_w_in, even_w_out, hgrn_lb, hgrn_onorm_g, attn_sinks, odd_w_in, odd_w_out, ret_onorm_g)
```
